```python
import math
import jax, jax.numpy as jnp
from jax import lax
import numpy as np

D_MODEL = 1024
BATCH = 8
SEQ = 4096
DEPTH = 4
DEC_BATCH = 8
DEC_SEQ = 16
PAST_LEN = 1024

CHUNK = 64
Q_BLOCK = 128
N_MIXERS = 3
N_A = (DEPTH + 2) // 3
N_B = (DEPTH + 1) // 3
N_C = DEPTH // 3
BRANCH = D_MODEL
A_HEAD_DIM = 64
A_HEADS = BRANCH // A_HEAD_DIM
B_HEAD_DIM = 64
B_HEADS = BRANCH // (2 * B_HEAD_DIM)
C_HEADS = 16
C_NOPE = 64
C_ROPE = 32
C_VDIM = BRANCH // C_HEADS
C_Q_LORA = D_MODEL // 2
C_KV_LORA = D_MODEL // 4
ROPE_THETA = 10000.0
EPS = 1e-6
NEG_INF = -1e30

kernel_name = 'hybrid_streaming_encoder_step'


def rms_norm(x, g):
    xf = x.astype(jnp.float32)
    y = xf * lax.rsqrt(jnp.mean(xf * xf, axis=-1, keepdims=True) + EPS)
    return (y * g.astype(jnp.float32)).astype(x.dtype)


def rope(x, pos):
    half = x.shape[-1] // 2
    inv = ROPE_THETA ** (-jnp.arange(half, dtype=jnp.float32) / half)
    ang = pos.astype(jnp.float32)[:, None] * inv
    bshape = (pos.shape[0],) + (1,) * (x.ndim - 3) + (half,)
    cos = jnp.cos(ang).reshape(bshape)
    sin = jnp.sin(ang).reshape(bshape)
    xf = x.astype(jnp.float32)
    x1, x2 = xf[..., :half], xf[..., half:]
    return jnp.concatenate([x1 * cos - x2 * sin, x2 * cos + x1 * sin], axis=-1).astype(x.dtype)


def masked_softmax(s, mask):
    return jax.nn.softmax(jnp.where(mask, s, NEG_INF), axis=-1)


def chunk_mask(q_pos, k_pos):
    return (k_pos[None, :] // CHUNK) <= (q_pos[:, None] // CHUNK)


def sweep_query_blocks(fn, q_pos, *qs):
    tq = q_pos.shape[0]
    blk = min(Q_BLOCK, tq)
    nb = tq // blk
    qs_b = tuple(jnp.moveaxis(q.reshape((q.shape[0], nb, blk) + q.shape[2:]), 1, 0) for q in qs)
    out = lax.map(lambda a: fn(a[0], *a[1]), (q_pos.reshape(nb, blk), qs_b))
    out = jnp.moveaxis(out, 0, 1)
    return out.reshape((out.shape[0], tq) + out.shape[3:])


def fox_mixer(h, k_past, v_past, lf_past, w_in, f_b, q_g, k_g):
    B, T, _ = h.shape
    P = k_past.shape[1]
    proj = h @ w_in
    q, k, v, f, z = jnp.split(proj, [BRANCH, 2 * BRANCH, 3 * BRANCH, 3 * BRANCH + A_HEADS], axis=-1)
    q = rms_norm(q.reshape(B, T, A_HEADS, A_HEAD_DIM), q_g)
    k = rms_norm(k.reshape(B, T, A_HEADS, A_HEAD_DIM), k_g)
    v = v.reshape(B, T, A_HEADS, A_HEAD_DIM)
    logf = jax.nn.log_sigmoid((f + f_b).astype(jnp.float32))
    k_all = jnp.concatenate([k_past.astype(k.dtype), k], axis=1)
    v_all = jnp.concatenate([v_past.astype(v.dtype), v], axis=1)
    cum = jnp.cumsum(jnp.concatenate([lf_past.astype(jnp.float32), logf], axis=1), axis=1)
    k_pos = jnp.arange(P + T)
    q_pos = P + jnp.arange(T)
    k_cum = jnp.transpose(cum, (0, 2, 1))[:, :, None, :]
    scale = A_HEAD_DIM ** -0.5

    def block(qp, qb, qc):
        s = jnp.einsum('bqhd,bkhd->bhqk', qb, k_all, preferred_element_type=jnp.float32) * scale
        s = s + jnp.transpose(qc, (0, 2, 1))[..., None] - k_cum
        p = masked_softmax(s, k_pos[None, :] <= qp[:, None])
        return jnp.einsum('bhqk,bkhd->bqhd', p.astype(v_all.dtype), v_all)

    o = sweep_query_blocks(block, q_pos, q, cum[:, P:])
    u = o.reshape(B, T, BRANCH) * jax.nn.silu(z)
    return u, (k, v, logf)


def diff_mixer(h, k_past, v_past, w_in, q_g, k_g, lam_q1, lam_k1, lam_q2, lam_k2, sub_g, layer_idx):
    B, T, _ = h.shape
    P = k_past.shape[1]
    q_pos = P + jnp.arange(T)
    k_pos = jnp.arange(P + T)
    q, k, v, z = jnp.split(h @ w_in, 4, axis=-1)
    q = rope(rms_norm(q.reshape(B, T, B_HEADS, 2, B_HEAD_DIM), q_g), q_pos)
    k = rope(rms_norm(k.reshape(B, T, B_HEADS, 2, B_HEAD_DIM), k_g), q_pos)
    v = v.reshape(B, T, B_HEADS, 2 * B_HEAD_DIM)
    k_all = jnp.concatenate([k_past.astype(k.dtype), k], axis=1)
    v_all = jnp.concatenate([v_past.astype(v.dtype), v], axis=1)
    lam_init = 0.8 - 0.6 * math.exp(-0.3 * layer_idx)
    lam = (jnp.exp(jnp.sum(lam_q1.astype(jnp.float32) * lam_k1.astype(jnp.float32)))
           - jnp.exp(jnp.sum(lam_q2.astype(jnp.float32) * lam_k2.astype(jnp.float32))) + lam_init)
    scale = B_HEAD_DIM ** -0.5

    def block(qp, qb):
        s = jnp.einsum('bqhcd,bkhcd->bhcqk', qb, k_all, preferred_element_type=jnp.float32) * scale
        p = masked_softmax(s, chunk_mask(qp, k_pos))
        pd = p[:, :, 0] - lam * p[:, :, 1]
        return jnp.einsum('bhqk,bkhe->bqhe', pd.astype(v_all.dtype), v_all)

    o = sweep_query_blocks(block, q_pos, q)
    o = rms_norm(o, sub_g) * (1.0 - lam_init)
    u = o.reshape(B, T, BRANCH) * jax.nn.silu(z)
    return u, (k, v)


def mla_mixer(h, kv_past, kpe_past, w_in, qa_g, kva_g, qb_w, kvb_w, q_g, k_g):
    B, T, _ = h.shape
    P = kv_past.shape[1]
    Tk = P + T
    q_pos = P + jnp.arange(T)
    k_pos = jnp.arange(Tk)
    qa, kva, kpe, z = jnp.split(h @ w_in, [C_Q_LORA, C_Q_LORA + C_KV_LORA, C_Q_LORA + C_KV_LORA + C_ROPE], axis=-1)
    q = (rms_norm(qa, qa_g) @ qb_w).reshape(B, T, C_HEADS, C_NOPE + C_ROPE)
    kv_lat = rms_norm(kva, kva_g)
    kv_all = jnp.concatenate([kv_past.astype(kv_lat.dtype), kv_lat], axis=1)
    kpe_all = jnp.concatenate([kpe_past.astype(kpe.dtype), kpe], axis=1)
    kvb = (kv_all @ kvb_w).reshape(B, Tk, C_HEADS, C_NOPE + C_VDIM)
    k_nope, v_all = kvb[..., :C_NOPE], kvb[..., C_NOPE:]
    k_all = jnp.concatenate([k_nope, jnp.broadcast_to(kpe_all[:, :, None, :], (B, Tk, C_HEADS, C_ROPE))], axis=-1)
    q = rms_norm(q, q_g)
    k_all = rms_norm(k_all, k_g)
    q = jnp.concatenate([q[..., :C_NOPE], rope(q[..., C_NOPE:], q_pos)], axis=-1)
    k_all = jnp.concatenate([k_all[..., :C_NOPE], rope(k_all[..., C_NOPE:], k_pos)], axis=-1)
    scale = (C_NOPE + C_ROPE) ** -0.5

    def block(qp, qb):
        s = jnp.einsum('bqhd,bkhd->bhqk', qb, k_all, preferred_element_type=jnp.float32) * scale
        p = masked_softmax(s, chunk_mask(qp, k_pos))
        return jnp.einsum('bhqk,bkhd->bqhd', p.astype(v_all.dtype), v_all)

    o = sweep_query_blocks(block, q_pos, q)
    u = o.reshape(B, T, BRANCH) * jax.nn.silu(z)
    return u, (kv_lat, kpe)


def run_trunk(x, c, past_a, past_b, past_c, shared, pa, pb, pc):
    norm_g, ada_w, ada_b, out_w = shared
    new_a, new_b, new_c = [], [], []
    cond = jax.nn.silu(c)
    for i in range(DEPTH):
        j = i // N_MIXERS
        shift, scale, gate = jnp.split(cond @ ada_w[i] + ada_b[i], 3, axis=-1)
        h = rms_norm(x, norm_g[i]) * (1.0 + scale[:, None, :]) + shift[:, None, :]
        if i % N_MIXERS == 0:
            u, rows = fox_mixer(h, *(t[j] for t in past_a), *(w[j] for w in pa))
            new_a.append(rows)
        elif i % N_MIXERS == 1:
            u, rows = diff_mixer(h, *(t[j] for t in past_b), *(w[j] for w in pb), i)
            new_b.append(rows)
        else:
            u, rows = mla_mixer(h, *(t[j] for t in past_c), *(w[j] for w in pc))
            new_c.append(rows)
        x = x + gate[:, None, :] * (u @ out_w[i])
    stack = lambda rows: tuple(jnp.stack(r) for r in zip(*rows))
    return x, stack(new_a), stack(new_b), stack(new_c)


def setup_inputs(seed: int = 0) -> dict:
    key = jax.random.key(seed)
    ks = iter(jax.random.split(key, 48))

    def nrm(shape, s=1.0):
        return s * jax.random.normal(next(ks), shape, jnp.float32)

    def gain(shape):
        return 1.0 + 0.05 * jax.random.normal(next(ks), shape, jnp.float32)

    d = {}
    d['x_prompt'] = nrm((BATCH, SEQ, D_MODEL))
    d['x_sample'] = nrm((DEC_BATCH, DEC_SEQ, D_MODEL))
    d['cache_a_k'] = nrm((N_A, DEC_BATCH, PAST_LEN, A_HEADS, A_HEAD_DIM))
    d['cache_a_v'] = nrm((N_A, DEC_BATCH, PAST_LEN, A_HEADS, A_HEAD_DIM))
    d['cache_a_logf'] = jax.nn.log_sigmoid(3.0 + nrm((N_A, DEC_BATCH, PAST_LEN, A_HEADS)))
    d['cache_b_k'] = nrm((N_B, DEC_BATCH, PAST_LEN, B_HEADS, 2, B_HEAD_DIM))
    d['cache_b_v'] = nrm((N_B, DEC_BATCH, PAST_LEN, B_HEADS, 2 * B_HEAD_DIM))
    d['cache_c_kv'] = nrm((N_C, DEC_BATCH, PAST_LEN, C_KV_LORA))
    d['cache_c_kpe'] = nrm((N_C, DEC_BATCH, PAST_LEN, C_ROPE))
    d['c_prompt'] = nrm((BATCH, D_MODEL))
    d['c_sample'] = nrm((DEC_BATCH, D_MODEL))
    d['norm_g'] = gain((DEPTH, D_MODEL))
    d['ada_w'] = nrm((DEPTH, D_MODEL, 3 * D_MODEL), 0.5 * D_MODEL ** -0.5)
    d['ada_b'] = nrm((DEPTH, 3 * D_MODEL), 0.02)
    d['out_w'] = nrm((DEPTH, BRANCH, D_MODEL), BRANCH ** -0.5)
    d['a_in_w'] = nrm((N_A, D_MODEL, 4 * BRANCH + A_HEADS), D_MODEL ** -0.5)
    d['a_f_b'] = 3.0 + nrm((N_A, A_HEADS), 0.5)
    d['a_q_g'] = gain((N_A, A_HEAD_DIM))
    d['a_k_g'] = gain((N_A, A_HEAD_DIM))
    d['b_in_w'] = nrm((N_B, D_MODEL, 4 * BRANCH), D_MODEL ** -0.5)
    d['b_q_g'] = gain((N_B, 2, B_HEAD_DIM))
    d['b_k_g'] = gain((N_B, 2, B_HEAD_DIM))
    d['b_lam_q1'] = nrm((N_B, B_HEAD_DIM), 0.1)
    d['b_lam_k1'] = nrm((N_B, B_HEAD_DIM), 0.1)
    d['b_lam_q2'] = nrm((N_B, B_HEAD_DIM), 0.1)
    d['b_lam_k2'] = nrm((N_B, B_HEAD_DIM), 0.1)
    d['b_sub_g'] = gain((N_B, 2 * B_HEAD_DIM))
    d['c_in_w'] = nrm((N_C, D_MODEL, C_Q_LORA + C_KV_LORA + C_ROPE + BRANCH), D_MODEL ** -0.5)
    d['c_qa_g'] = gain((N_C, C_Q_LORA))
    d['c_kva_g'] = gain((N_C, C_KV_LORA))
    d['c_qb_w'] = nrm((N_C, C_Q_LORA, C_HEADS * (C_NOPE + C_ROPE)), C_Q_LORA ** -0.5)
    d['c_kvb_w'] = nrm((N_C, C_KV_LORA, C_HEADS * (C_NOPE + C_VDIM)), C_KV_LORA ** -0.5)
    d['c_q_g'] = gain((N_C, C_NOPE + C_ROPE))
    d['c_k_g'] = gain((N_C, C_NOPE + C_ROPE))
    return d


def reference(x_prompt, x_sample, cache_a_k, cache_a_v, cache_a_logf, cache_b_k, cache_b_v,
              cache_c_kv, cache_c_kpe, c_prompt, c_sample, norm_g, ada_w, ada_b, out_w,
              a_in_w, a_f_b, a_q_g, a_k_g, b_in_w, b_q_g, b_k_g, b_lam_q1, b_lam_k1,
              b_lam_q2, b_lam_k2, b_sub_g, c_in_w, c_qa_g, c_kva_g, c_qb_w, c_kvb_w, c_q_g, c_k_g):
    shared = (norm_g, ada_w, ada_b, out_w)
    pa = (a_in_w, a_f_b, a_q_g, a_k_g)
    pb = (b_in_w, b_q_g, b_k_g, b_lam_q1, b_lam_k1, b_lam_q2, b_lam_k2, b_sub_g)
    pc = (c_in_w, c_qa_g, c_kva_g, c_qb_w, c_kvb_w, c_q_g, c_k_g)
    bp = x_prompt.shape[0]
    dt = x_prompt.dtype
    empty_a = (jnp.zeros((N_A, bp, 0, A_HEADS, A_HEAD_DIM), dt),
               jnp.zeros((N_A, bp, 0, A_HEADS, A_HEAD_DIM), dt),
               jnp.zeros((N_A, bp, 0, A_HEADS), jnp.float32))
    empty_b = (jnp.zeros((N_B, bp, 0, B_HEADS, 2, B_HEAD_DIM), dt),
               jnp.zeros((N_B, bp, 0, B_HEADS, 2 * B_HEAD_DIM), dt))
    empty_c = (jnp.zeros((N_C, bp, 0, C_KV_LORA), dt),
               jnp.zeros((N_C, bp, 0, C_ROPE), dt))
    y_prompt, (a_k_p, a_v_p, a_logf_p), (b_k_p, b_v_p), (c_kv_p, c_kpe_p) = run_trunk(
        x_prompt, c_prompt, empty_a, empty_b, empty_c, shared, pa, pb, pc)
    y_sample, (a_k_s, a_v_s, a_logf_s), (b_k_s, b_v_s), (c_kv_s, c_kpe_s) = run_trunk(
        x_sample, c_sample, (cache_a_k, cache_a_v, cache_a_logf), (cache_b_k, cache_b_v),
        (cache_c_kv, cache_c_kpe), shared, pa, pb, pc)
    return (y_prompt, y_sample, a_k_p, a_v_p, a_logf_p, b_k_p, b_v_p, c_kv_p, c_kpe_p,
            a_k_s, a_v_s, a_logf_s, b_k_s, b_v_s, c_kv_s, c_kpe_s)
```

```python
import functools
import math

import numpy as np
import jax
import jax.numpy as jnp
from jax import lax
from jax.experimental import pallas as pl
from jax.experimental.pallas import tpu as pltpu

F32 = jnp.float32
BF16 = jnp.bfloat16

CHUNK = 64
CHUNK_SHIFT = 6
A_HEAD_DIM = 64
B_HEAD_DIM = 64
C_HEADS = 16
C_NOPE = 64
C_ROPE = 32
C_VDIM = 64
ROPE_THETA = 10000.0
EPS = 1e-6
NEG_INF = -1e30
LOG2E = 1.4426950408889634

LANES = 128
MXU_DIM = 256
V7X_VMEM_BYTES = 64 * 1024 * 1024
VMEM_LIMIT = V7X_VMEM_BYTES * 7 // 8

ROW_TILE = 512
ATTN_TILE = 512
DEC_KEY_TILE = 128
CUMSUM_TILE = 512


def _cparams(*sem):
    return pltpu.CompilerParams(dimension_semantics=sem, vmem_limit_bytes=VMEM_LIMIT)


def _dot(a, b):
    return jnp.dot(a, b, preferred_element_type=F32)


def _split_bf16(x):
    hi = x.astype(BF16)
    lo = (x - hi.astype(F32)).astype(BF16)
    return hi, lo


def _group_sumsq(x, bd):
    outs = []
    for c in range(x.shape[1] // MXU_DIM):
        xc = x[:, c * MXU_DIM:(c + 1) * MXU_DIM]
        hi, lo = _split_bf16(xc * xc)
        outs.append(_dot(hi, bd) + _dot(lo, bd))
    return outs[0] if len(outs) == 1 else jnp.concatenate(outs, axis=1)


def _modulated_norm(x, g, scale, shift):
    y = x * lax.rsqrt(jnp.mean(x * x, axis=-1, keepdims=True) + EPS) * g
    return y * (1.0 + scale) + shift


def _row_norm(x, g):
    return x * lax.rsqrt(jnp.mean(x * x, axis=-1, keepdims=True) + EPS) * g


def _swap_halves(x, half):
    lane = lax.broadcasted_iota(jnp.int32, (1, LANES), 1)
    first = (lane & (2 * half - 1)) < half
    return jnp.where(first, pltpu.roll(x, LANES - half, 1), pltpu.roll(x, half, 1))


def _rope_lanes(x, cos, sin, half):
    outs = []
    for c in range(x.shape[1] // LANES):
        xc = x[:, c * LANES:(c + 1) * LANES]
        outs.append(xc * cos + _swap_halves(xc, half) * sin)
    return outs[0] if len(outs) == 1 else jnp.concatenate(outs, axis=1)


def _silu(z):
    return z * jax.nn.sigmoid(z)


def _log_sigmoid(x):
    return jnp.minimum(x, 0.0) - jnp.log1p(jnp.exp(-jnp.abs(x)))


def _ada_kernel(c_ref, w_ref, b_ref, o_ref):
    c = c_ref[...]
    c_hi, c_lo = _split_bf16(_silu(c))
    w_hi, w_lo = _split_bf16(w_ref[...])
    o_ref[...] = _dot(c_hi, w_hi) + _dot(c_hi, w_lo) + _dot(c_lo, w_hi) + b_ref[...]


def _ada(c_all, ada_w, ada_b):
    depth, d, n3 = ada_w.shape
    rows = c_all.shape[0]
    tn = d
    return pl.pallas_call(
        _ada_kernel,
        grid=(depth, n3 // tn),
        in_specs=[
            pl.BlockSpec((rows, d), lambda l, j: (0, 0)),
            pl.BlockSpec((None, d, tn), lambda l, j: (l, 0, j)),
            pl.BlockSpec((None, 1, tn), lambda l, j: (l, 0, j)),
        ],
        out_specs=pl.BlockSpec((None, rows, tn), lambda l, j: (l, 0, j)),
        out_shape=jax.ShapeDtypeStruct((depth, rows, n3), F32),
        compiler_params=_cparams("arbitrary", "arbitrary"),
        name="ada_modulation",
    )(c_all, ada_w, ada_b.reshape(depth, 1, n3))


def _row_spec(tm, n):
    return pl.BlockSpec((None, tm, n), lambda b, i: (b, i, 0))


def _batch_spec(n):
    return pl.BlockSpec((None, 1, n), lambda b, i: (b, 0, 0))


def _const_spec(shape):
    return pl.BlockSpec(shape, lambda b, i: (0,) * len(shape))


def _fox_proj_kernel(x_ref, g_ref, sc_ref, sh_ref, w_ref, fb_ref, qg_ref, kg_ref, bd_ref,
                     k_out, v_out, lf_out, qb_out, kb_out, vb_out, zs_out, *, qscale):
    d = x_ref.shape[1]
    h = _modulated_norm(x_ref[...], g_ref[...], sc_ref[...], sh_ref[...]).astype(BF16)
    bd = bd_ref[...]
    inv_dim = 1.0 / A_HEAD_DIM

    q = _dot(h, w_ref[:, 0:d])
    q = q * lax.rsqrt(_group_sumsq(q, bd) * inv_dim + EPS) * qg_ref[...]
    qb_out[...] = (q * qscale).astype(BF16)

    k = _dot(h, w_ref[:, d:2 * d])
    k = k * lax.rsqrt(_group_sumsq(k, bd) * inv_dim + EPS) * kg_ref[...]
    k_out[...] = k
    kb_out[...] = k.astype(BF16)

    v = _dot(h, w_ref[:, 2 * d:3 * d])
    v_out[...] = v
    vb_out[...] = v.astype(BF16)

    zs_out[...] = _silu(_dot(h, w_ref[:, 3 * d:4 * d])).astype(BF16)

    f = _dot(h, w_ref[:, 4 * d:4 * d + LANES]) + fb_ref[...]
    lf_out[...] = _log_sigmoid(f)[:, :lf_out.shape[1]]


def _fox_proj(x, g, scale, shift, w, fb, qg, kg, bd, qscale):
    b, t, d = x.shape
    heads = d // A_HEAD_DIM
    tm = min(ROW_TILE, t)
    f32_out = jax.ShapeDtypeStruct((b, t, d), F32)
    bf_out = jax.ShapeDtypeStruct((b, t, d), BF16)
    return pl.pallas_call(
        functools.partial(_fox_proj_kernel, qscale=qscale),
        grid=(b, t // tm),
        in_specs=[_row_spec(tm, d), _const_spec((1, d)), _batch_spec(d), _batch_spec(d),
                  _const_spec(w.shape), _const_spec((1, LANES)), _const_spec((1, d)),
                  _const_spec((1, d)), _const_spec((MXU_DIM, MXU_DIM))],
        out_specs=[_row_spec(tm, d), _row_spec(tm, d), _row_spec(tm, heads),
                   _row_spec(tm, d), _row_spec(tm, d), _row_spec(tm, d), _row_spec(tm, d)],
        out_shape=[f32_out, f32_out, jax.ShapeDtypeStruct((b, t, heads), F32),
                   bf_out, bf_out, bf_out, bf_out],
        compiler_params=_cparams("parallel", "arbitrary"),
        name="fox_in_proj",
    )(x, g, scale, shift, w, fb, qg, kg, bd)


def _diff_proj_kernel(x_ref, g_ref, sc_ref, sh_ref, w_ref, qg_ref, kg_ref, bd_ref, cos_ref, sin_ref,
                      k_out, v_out, qb_out, kb_out, vb_out, zs_out, *, qscale):
    d = x_ref.shape[1]
    h = _modulated_norm(x_ref[...], g_ref[...], sc_ref[...], sh_ref[...]).astype(BF16)
    bd = bd_ref[...]
    cos = cos_ref[...]
    sin = sin_ref[...]
    inv_dim = 1.0 / B_HEAD_DIM
    half = B_HEAD_DIM // 2

    q = _dot(h, w_ref[:, 0:d])
    q = q * lax.rsqrt(_group_sumsq(q, bd) * inv_dim + EPS) * qg_ref[...]
    qb_out[...] = (_rope_lanes(q, cos, sin, half) * qscale).astype(BF16)

    k = _dot(h, w_ref[:, d:2 * d])
    k = k * lax.rsqrt(_group_sumsq(k, bd) * inv_dim + EPS) * kg_ref[...]
    k = _rope_lanes(k, cos, sin, half)
    k_out[...] = k
    kb_out[...] = k.astype(BF16)

    v = _dot(h, w_ref[:, 2 * d:3 * d])
    v_out[...] = v
    vb_out[...] = v.astype(BF16)

    zs_out[...] = _silu(_dot(h, w_ref[:, 3 * d:4 * d])).astype(BF16)


def _diff_proj(x, g, scale, shift, w, qg, kg, bd, cos, sin, qscale):
    b, t, d = x.shape
    tm = min(ROW_TILE, t)
    f32_out = jax.ShapeDtypeStruct((b, t, d), F32)
    bf_out = jax.ShapeDtypeStruct((b, t, d), BF16)
    table_spec = pl.BlockSpec((tm, LANES), lambda b_, i: (i, 0))
    return pl.pallas_call(
        functools.partial(_diff_proj_kernel, qscale=qscale),
        grid=(b, t // tm),
        in_specs=[_row_spec(tm, d), _const_spec((1, d)), _batch_spec(d), _batch_spec(d),
                  _const_spec(w.shape), _const_spec((1, d)), _const_spec((1, d)),
                  _const_spec((MXU_DIM, MXU_DIM)), table_spec, table_spec],
        out_specs=[_row_spec(tm, d)] * 6,
        out_shape=[f32_out, f32_out, bf_out, bf_out, bf_out, bf_out],
        compiler_params=_cparams("parallel", "arbitrary"),
        name="diff_in_proj",
    )(x, g, scale, shift, w, qg, kg, bd, cos, sin)


def _mla_proj_kernel(x_ref, g_ref, sc_ref, sh_ref, w_ref, qag_ref, kvag_ref, qbw_ref, qg_ref,
                     bd_ref, cos_ref, sin_ref,
                     kv_out, kpe_out, qb_out, zs_out, *, qscale, q_lora, kv_lora):
    d = x_ref.shape[1]
    h = _modulated_norm(x_ref[...], g_ref[...], sc_ref[...], sh_ref[...]).astype(BF16)
    n_a = q_lora + kv_lora + LANES

    a = _dot(h, w_ref[:, 0:n_a])
    qa = _row_norm(a[:, 0:q_lora], qag_ref[...]).astype(BF16)
    kv_out[...] = _row_norm(a[:, q_lora:q_lora + kv_lora], kvag_ref[...])
    kpe_out[...] = a[:, q_lora + kv_lora:q_lora + kv_lora + kpe_out.shape[1]]

    zs_out[...] = _silu(_dot(h, w_ref[:, n_a:n_a + d])).astype(BF16)

    q = _dot(qa, qbw_ref[...])
    inv_dim = 1.0 / (C_NOPE + C_ROPE)
    q = q * lax.rsqrt(_group_sumsq(q, bd_ref[...]) * inv_dim + EPS) * qg_ref[...]
    qb_out[...] = (_rope_lanes(q, cos_ref[...], sin_ref[...], C_ROPE // 2) * qscale).astype(BF16)


def _mla_proj(x, g, scale, shift, w, qag, kvag, qbw, qg, bd, cos, sin, qscale):
    b, t, d = x.shape
    tm = min(ROW_TILE, t)
    q_lora = qag.shape[1]
    kv_lora = kvag.shape[1]
    nslab = qbw.shape[1]
    table_spec = pl.BlockSpec((tm, LANES), lambda b_, i: (i, 0))
    return pl.pallas_call(
        functools.partial(_mla_proj_kernel, qscale=qscale, q_lora=q_lora, kv_lora=kv_lora),
        grid=(b, t // tm),
        in_specs=[_row_spec(tm, d), _const_spec((1, d)), _batch_spec(d), _batch_spec(d),
                  _const_spec(w.shape), _const_spec((1, q_lora)), _const_spec((1, kv_lora)),
                  _const_spec(qbw.shape), _const_spec((1, nslab)),
                  _const_spec((MXU_DIM, MXU_DIM)), table_spec, table_spec],
        out_specs=[_row_spec(tm, kv_lora), _row_spec(tm, C_ROPE), _row_spec(tm, nslab),
                   _row_spec(tm, d)],
        out_shape=[jax.ShapeDtypeStruct((b, t, kv_lora), F32),
                   jax.ShapeDtypeStruct((b, t, C_ROPE), F32),
                   jax.ShapeDtypeStruct((b, t, nslab), BF16),
                   jax.ShapeDtypeStruct((b, t, d), BF16)],
        compiler_params=_cparams("parallel", "arbitrary"),
        name="mla_in_proj",
    )(x, g, scale, shift, w, qag, kvag, qbw, qg, bd, cos, sin)


def _mla_key_kernel(kv_ref, kpe_ref, wk_ref, wv_ref, kg_ref, bd_ref, cos_ref, sin_ref,
                    kb_out, vb_out):
    kv = kv_ref[...].astype(BF16)
    vb_out[...] = _dot(kv, wv_ref[...]).astype(BF16)
    kn = _dot(kv, wk_ref[...])
    kpe = kpe_ref[...]
    k = jnp.concatenate([kn[:, c * LANES:(c + 1) * LANES] + kpe
                         for c in range(kn.shape[1] // LANES)], axis=1)
    inv_dim = 1.0 / (C_NOPE + C_ROPE)
    k = k * lax.rsqrt(_group_sumsq(k, bd_ref[...]) * inv_dim + EPS) * kg_ref[...]
    kb_out[...] = _rope_lanes(k, cos_ref[...], sin_ref[...], C_ROPE // 2).astype(BF16)


def _mla_key(kv_all, kpe_all, wk, wv, kg, bd, cos, sin):
    b, tk, kv_lora = kv_all.shape
    tm = min(ROW_TILE, tk)
    nslab = wk.shape[1]
    nv = wv.shape[1]
    table_spec = pl.BlockSpec((tm, LANES), lambda b_, i: (i, 0))
    return pl.pallas_call(
        _mla_key_kernel,
        grid=(b, tk // tm),
        in_specs=[_row_spec(tm, kv_lora), _row_spec(tm, LANES), _const_spec(wk.shape),
                  _const_spec(wv.shape), _const_spec((1, nslab)),
                  _const_spec((MXU_DIM, MXU_DIM)), table_spec, table_spec],
        out_specs=[_row_spec(tm, nslab), _row_spec(tm, nv)],
        out_shape=[jax.ShapeDtypeStruct((b, tk, nslab), BF16),
                   jax.ShapeDtypeStruct((b, tk, nv), BF16)],
        compiler_params=_cparams("parallel", "arbitrary"),
        name="mla_key_proj",
    )(kv_all, kpe_all, wk, wv, kg, bd, cos, sin)


def _out_proj_kernel(x_ref, u_ref, w_ref, gate_ref, o_ref):
    o_ref[...] = x_ref[...] + gate_ref[...] * _dot(u_ref[...], w_ref[...])


def _out_proj(x, u, w, gate):
    b, t, d = x.shape
    tm = min(ROW_TILE, t)
    return pl.pallas_call(
        _out_proj_kernel,
        grid=(b, t // tm),
        in_specs=[_row_spec(tm, d), _row_spec(tm, d), _const_spec(w.shape), _batch_spec(d)],
        out_specs=_row_spec(tm, d),
        out_shape=jax.ShapeDtypeStruct((b, t, d), F32),
        compiler_params=_cparams("parallel", "arbitrary"),
        name="out_proj_residual",
    )(x, u, w, gate)


def _cumsum_kernel(x_ref, u_ref, o_ref, carry_ref):
    @pl.when(pl.program_id(1) == 0)
    def _():
        carry_ref[...] = jnp.zeros_like(carry_ref)

    x = x_ref[...]
    hi = x.astype(BF16)
    r = x - hi.astype(F32)
    mid = r.astype(BF16)
    lo = (r - mid.astype(F32)).astype(BF16)
    u = u_ref[...]
    c = _dot(hi, u) + _dot(mid, u) + _dot(lo, u) + carry_ref[...]
    o_ref[...] = c
    carry_ref[...] = c[:, c.shape[1] - 1:]


def _cumsum_lanes(xt):
    b, hh, t = xt.shape
    tc = CUMSUM_TILE if t % CUMSUM_TILE == 0 else t
    tri = jnp.asarray(np.triu(np.ones((tc, tc), np.float32)), BF16)
    return pl.pallas_call(
        _cumsum_kernel,
        grid=(b, t // tc),
        in_specs=[pl.BlockSpec((None, hh, tc), lambda b_, i: (b_, 0, i)),
                  pl.BlockSpec((tc, tc), lambda b_, i: (0, 0))],
        out_specs=pl.BlockSpec((None, hh, tc), lambda b_, i: (b_, 0, i)),
        out_shape=jax.ShapeDtypeStruct((b, hh, t), F32),
        scratch_shapes=[pltpu.VMEM((hh, 1), F32)],
        compiler_params=_cparams("parallel", "arbitrary"),
        name="logf_cumsum",
    )(xt, tri)


def _attn_kernel(*refs, mode, tq, tk, nkb, past, tk_valid, lam_init):
    if mode == "fox":
        q_ref, k_ref, v_ref, zs_ref, qc_ref, kc_ref, o_ref, m_ref, l_ref, acc_ref = refs
    elif mode == "diff":
        q_ref, k_ref, v_ref, zs_ref, lam_ref, subg_ref, o_ref, m_ref, l_ref, acc_ref = refs
    else:
        q_ref, k_ref, v_ref, zs_ref, o_ref, m_ref, l_ref, acc_ref = refs

    q0 = pl.program_id(2) * tq
    lane = lax.broadcasted_iota(jnp.int32, (1, LANES), 1)
    low = lane < (LANES // 2)

    if mode == "mla":
        qs = (q_ref[:, 0:LANES], q_ref[:, LANES:2 * LANES])
    else:
        q = q_ref[...]
        zero = jnp.zeros_like(q)
        qs = (jnp.where(low, q, zero), jnp.where(low, zero, q))

    if mode == "fox":
        pair = pl.program_id(1)
        qc = qc_ref[...] * LOG2E
        head_lane = lax.broadcasted_iota(jnp.int32, (1, qc.shape[1]), 1)
        qcl = tuple(jnp.sum(jnp.where(head_lane == 2 * pair + h, qc, 0.0), axis=-1, keepdims=True)
                    for h in range(2))
    else:
        qcl = (None, None)

    m_ref[...] = jnp.full(m_ref.shape, NEG_INF, F32)
    l_ref[...] = jnp.zeros(l_ref.shape, F32)
    acc_ref[...] = jnp.zeros(acc_ref.shape, F32)

    q_pos = past + q0 + lax.broadcasted_iota(jnp.int32, (tq, 1), 0)

    def step(j, masked):
        k0 = pl.multiple_of(j * tk, tk)
        v = v_ref[pl.ds(k0, tk), :]
        if mode == "mla":
            ks = (k_ref[pl.ds(k0, tk), 0:LANES], k_ref[pl.ds(k0, tk), LANES:2 * LANES])
        else:
            kk = k_ref[pl.ds(k0, tk), :]
            ks = (kk, kk)
        if masked:
            k_pos = k0 + lax.broadcasted_iota(jnp.int32, (1, tk), 1)
            if mode == "fox":
                mask = k_pos <= q_pos
            else:
                mask = (k_pos >> CHUNK_SHIFT) <= (q_pos >> CHUNK_SHIFT)
                if tk_valid < nkb * tk:
                    mask = jnp.logical_and(mask, k_pos < tk_valid)
        if mode == "fox":
            kcl = kc_ref[j] * LOG2E
        for h in range(2):
            s = lax.dot_general(qs[h], ks[h], (((1,), (1,)), ((), ())),
                                preferred_element_type=F32)
            if mode == "fox":
                s = s - kcl[h:h + 1, :]
            if masked:
                s = jnp.where(mask, s, NEG_INF)
            m_old = m_ref[h]
            row_max = jnp.max(s, axis=-1, keepdims=True)
            if mode == "fox":
                m_new = jnp.maximum(m_old, row_max + qcl[h])
                p = jnp.exp2(s - (m_new - qcl[h]))
            else:
                m_new = jnp.maximum(m_old, row_max)
                p = jnp.exp2(s - m_new)
            alpha = jnp.exp2(m_old - m_new)
            l_ref[h] = alpha * l_ref[h] + jnp.sum(p, axis=-1, keepdims=True)
            acc_ref[h] = alpha * acc_ref[h] + _dot(p.astype(BF16), v)
            m_ref[h] = m_new

    if mode == "fox":
        n_full = (past + q0 + 1) // tk
        j_last = (past + q0 + tq - 1) // tk
    else:
        n_full = ((past + q0) // CHUNK + 1) * CHUNK // tk
        j_last = ((past + q0 + tq - 1) // CHUNK * CHUNK + CHUNK - 1) // tk
    n_full = jnp.minimum(n_full, tk_valid // tk)
    j_last = jnp.minimum(j_last, nkb - 1)

    def full_body(j, carry):
        step(j, False)
        return carry

    def masked_body(j, carry):
        step(j, True)
        return carry

    lax.fori_loop(0, n_full, full_body, 0)
    lax.fori_loop(n_full, j_last + 1, masked_body, 0)

    o0 = acc_ref[0] / l_ref[0]
    o1 = acc_ref[1] / l_ref[1]
    if mode == "diff":
        lv = lam_ref[...]
        lam = (jnp.exp(jnp.sum(lv[0:1] * lv[1:2], axis=-1, keepdims=True))
               - jnp.exp(jnp.sum(lv[2:3] * lv[3:4], axis=-1, keepdims=True)) + lam_init)
        o = o0 - lam * o1
        o = _row_norm(o, subg_ref[...]) * (1.0 - lam_init)
    else:
        o = jnp.where(low, o0, o1)
    o_ref[...] = (o * zs_ref[...].astype(F32)).astype(BF16)


def _attention(mode, q, k_all, v_all, zs, extras, *, past, tk_valid, lam_init=0.0):
    b, t, d = zs.shape
    tk_total = k_all.shape[1]
    if past == 0:
        tq = min(ATTN_TILE, t)
        tk = tq
    else:
        tq = t
        tk = DEC_KEY_TILE
    nkb = tk_total // tk
    groups = d // LANES
    qw = q.shape[2] // groups

    in_specs = [
        pl.BlockSpec((None, tq, qw), lambda b_, g, i: (b_, i, g)),
        pl.BlockSpec((None, tk_total, qw), lambda b_, g, i: (b_, 0, g)),
        pl.BlockSpec((None, tk_total, LANES), lambda b_, g, i: (b_, 0, g)),
        pl.BlockSpec((None, tq, LANES), lambda b_, g, i: (b_, i, g)),
    ]
    if mode == "fox":
        qc, kc = extras
        heads = qc.shape[2]
        in_specs += [
            pl.BlockSpec((None, tq, heads), lambda b_, g, i: (b_, i, 0)),
            pl.BlockSpec((None, None, nkb, 2, tk), lambda b_, g, i: (b_, g, 0, 0, 0)),
        ]
    elif mode == "diff":
        lam_vecs, sub_g = extras
        in_specs += [
            pl.BlockSpec(lam_vecs.shape, lambda b_, g, i: (0, 0)),
            pl.BlockSpec(sub_g.shape, lambda b_, g, i: (0, 0)),
        ]
    kern = functools.partial(_attn_kernel, mode=mode, tq=tq, tk=tk, nkb=nkb, past=past,
                             tk_valid=tk_valid, lam_init=lam_init)
    return pl.pallas_call(
        kern,
        grid=(b, groups, t // tq),
        in_specs=in_specs,
        out_specs=pl.BlockSpec((None, tq, LANES), lambda b_, g, i: (b_, i, g)),
        out_shape=jax.ShapeDtypeStruct((b, t, d), BF16),
        scratch_shapes=[pltpu.VMEM((2, tq, 1), F32), pltpu.VMEM((2, tq, 1), F32),
                        pltpu.VMEM((2, tq, LANES), F32)],
        compiler_params=_cparams("parallel", "parallel", "arbitrary"),
        name=mode + "_attention",
    )(q, k_all, v_all, zs, *extras)


def _group_matrix(group):
    idx = np.arange(MXU_DIM) // group
    return jnp.asarray((idx[:, None] == idx[None, :]).astype(np.float32), BF16)


def _rope_tables(pos, half, lead, trail):
    inv = ROPE_THETA ** (-jnp.arange(half, dtype=F32) / half)
    ang = pos.astype(F32)[:, None] * inv
    cos = jnp.cos(ang)
    sin = jnp.sin(ang)
    reps = (LANES - lead - trail) // (2 * half)
    n = pos.shape[0]
    cos_t = jnp.concatenate([jnp.ones((n, lead), F32)] + [cos, cos] * reps
                            + [jnp.ones((n, trail), F32)], axis=1)
    sin_t = jnp.concatenate([jnp.zeros((n, lead), F32)] + [-sin, sin] * reps
                            + [jnp.zeros((n, trail), F32)], axis=1)
    return cos_t, sin_t


def _pad_rows(x, rows):
    if x.shape[1] == rows:
        return x
    pad = [(0, 0)] * x.ndim
    pad[1] = (0, rows - x.shape[1])
    return jnp.pad(x, pad)


def _key_rows(past, t):
    if past == 0:
        return t
    return -(-(past + t) // DEC_KEY_TILE) * DEC_KEY_TILE


def _prep_weights(p):
    d = p["norm_g"].shape[1]
    w = {}
    a_in = p["a_in_w"]
    heads_a = d // A_HEAD_DIM
    w["a_in"] = jnp.concatenate(
        [a_in[:, :, :3 * d], a_in[:, :, 3 * d + heads_a:],
         jnp.pad(a_in[:, :, 3 * d:3 * d + heads_a], ((0, 0), (0, 0), (0, LANES - heads_a)))],
        axis=2).astype(BF16)
    w["a_fb"] = jnp.pad(p["a_f_b"], ((0, 0), (0, LANES - heads_a)))[:, None, :]
    w["a_qg"] = jnp.tile(p["a_q_g"], (1, heads_a))[:, None, :]
    w["a_kg"] = jnp.tile(p["a_k_g"], (1, heads_a))[:, None, :]

    heads_b = d // (2 * B_HEAD_DIM)
    w["b_in"] = p["b_in_w"].astype(BF16)
    nb = p["b_q_g"].shape[0]
    w["b_qg"] = jnp.tile(p["b_q_g"].reshape(nb, 2 * B_HEAD_DIM), (1, heads_b))[:, None, :]
    w["b_kg"] = jnp.tile(p["b_k_g"].reshape(nb, 2 * B_HEAD_DIM), (1, heads_b))[:, None, :]
    w["b_lam"] = jnp.stack([p["b_lam_q1"], p["b_lam_k1"], p["b_lam_q2"], p["b_lam_k2"]], axis=1)
    w["b_subg"] = p["b_sub_g"][:, None, :]

    c_in = p["c_in_w"]
    q_lora = p["c_qa_g"].shape[1]
    kv_lora = p["c_kva_g"].shape[1]
    n0 = q_lora + kv_lora
    w["c_in"] = jnp.concatenate(
        [c_in[:, :, :n0],
         jnp.pad(c_in[:, :, n0:n0 + C_ROPE], ((0, 0), (0, 0), (0, LANES - C_ROPE))),
         c_in[:, :, n0 + C_ROPE:]], axis=2).astype(BF16)
    w["c_qag"] = p["c_qa_g"][:, None, :]
    w["c_kvag"] = p["c_kva_g"][:, None, :]
    nc = c_in.shape[0]
    hd = C_NOPE + C_ROPE
    qb = p["c_qb_w"].reshape(nc, q_lora, C_HEADS, hd)
    w["c_qb"] = jnp.pad(qb, ((0, 0), (0, 0), (0, 0), (0, LANES - hd))).reshape(
        nc, q_lora, C_HEADS * LANES).astype(BF16)
    kvb = p["c_kvb_w"].reshape(nc, kv_lora, C_HEADS, C_NOPE + C_VDIM)
    w["c_kb"] = jnp.pad(kvb[..., :C_NOPE], ((0, 0), (0, 0), (0, 0), (0, LANES - C_NOPE))).reshape(
        nc, kv_lora, C_HEADS * LANES).astype(BF16)
    w["c_vb"] = kvb[..., C_NOPE:].reshape(nc, kv_lora, C_HEADS * C_VDIM).astype(BF16)
    slab_gain = lambda g: jnp.tile(jnp.pad(g, ((0, 0), (0, LANES - hd))), (1, C_HEADS))[:, None, :]
    w["c_qg"] = slab_gain(p["c_q_g"])
    w["c_kg"] = slab_gain(p["c_k_g"])
    w["out"] = p["out_w"].astype(BF16)
    return w


def _run_trunk(x, mods, caches, p, w, past):
    b, t, d = x.shape
    depth = p["norm_g"].shape[0]
    heads_a = d // A_HEAD_DIM
    tk_valid = past + t
    tk_total = _key_rows(past, t)
    q_pos = past + jnp.arange(t)
    k_pos = jnp.arange(tk_total)
    bd64 = _group_matrix(A_HEAD_DIM)
    bd128 = _group_matrix(LANES)
    new_a, new_b, new_c = [], [], []

    def with_past(cache, new_bf):
        if past == 0:
            return new_bf
        old = cache.reshape(b, past, -1).astype(BF16)
        return _pad_rows(jnp.concatenate([old, new_bf], axis=1), tk_total)

    for i in range(depth):
        j = i // 3
        shift = mods[i][:, None, 0:d]
        scale = mods[i][:, None, d:2 * d]
        gate = mods[i][:, None, 2 * d:3 * d]
        g = p["norm_g"][i][None, :]
        if i % 3 == 0:
            qscale = A_HEAD_DIM ** -0.5 * LOG2E
            k, v, logf, qb, kb, vb, zs = _fox_proj(
                x, g, scale, shift, w["a_in"][j], w["a_fb"][j], w["a_qg"][j], w["a_kg"][j],
                bd64, qscale)
            new_a.append((k.reshape(b, t, heads_a, A_HEAD_DIM),
                          v.reshape(b, t, heads_a, A_HEAD_DIM), logf))
            if past == 0:
                k_all, v_all, lf_all = kb, vb, logf
            else:
                k_all = with_past(caches[0][j], kb)
                v_all = with_past(caches[1][j], vb)
                lf_all = _pad_rows(jnp.concatenate([caches[2][j], logf], axis=1), tk_total)
            cum_t = _cumsum_lanes(jnp.transpose(lf_all, (0, 2, 1)))
            qc = jnp.transpose(cum_t[:, :, past:past + t], (0, 2, 1))
            tk = min(ATTN_TILE, t) if past == 0 else DEC_KEY_TILE
            kc = jnp.transpose(cum_t.reshape(b, heads_a // 2, 2, tk_total // tk, tk),
                               (0, 1, 3, 2, 4))
            u = _attention("fox", qb, k_all, v_all, zs, (qc, kc), past=past, tk_valid=tk_valid)
        elif i % 3 == 1:
            qscale = B_HEAD_DIM ** -0.5 * LOG2E
            cos, sin = _rope_tables(q_pos, B_HEAD_DIM // 2, 0, 0)
            k, v, qb, kb, vb, zs = _diff_proj(
                x, g, scale, shift, w["b_in"][j], w["b_qg"][j], w["b_kg"][j], bd64, cos, sin,
                qscale)
            heads_b = d // (2 * B_HEAD_DIM)
            new_b.append((k.reshape(b, t, heads_b, 2, B_HEAD_DIM),
                          v.reshape(b, t, heads_b, 2 * B_HEAD_DIM)))
            k_all = with_past(None if past == 0 else caches[3][j], kb)
            v_all = with_past(None if past == 0 else caches[4][j], vb)
            lam_init = 0.8 - 0.6 * math.exp(-0.3 * i)
            u = _attention("diff", qb, k_all, v_all, zs, (w["b_lam"][j], w["b_subg"][j]),
                           past=past, tk_valid=tk_valid, lam_init=lam_init)
        else:
            qscale = (C_NOPE + C_ROPE) ** -0.5 * LOG2E
            cos_q, sin_q = _rope_tables(q_pos, C_ROPE // 2, C_NOPE, LANES - C_NOPE - C_ROPE)
            cos_k, sin_k = _rope_tables(k_pos, C_ROPE // 2, C_NOPE, LANES - C_NOPE - C_ROPE)
            kv_lat, kpe, qb, zs = _mla_proj(
                x, g, scale, shift, w["c_in"][j], w["c_qag"][j], w["c_kvag"][j], w["c_qb"][j],
                w["c_qg"][j], bd128, cos_q, sin_q, qscale)
            new_c.append((kv_lat, kpe))
            if past == 0:
                kv_all, kpe_all = kv_lat, kpe
            else:
                kv_all = _pad_rows(jnp.concatenate([caches[5][j], kv_lat], axis=1), tk_total)
                kpe_all = _pad_rows(jnp.concatenate([caches[6][j], kpe], axis=1), tk_total)
            kpe_slab = jnp.pad(kpe_all, ((0, 0), (0, 0), (C_NOPE, LANES - C_NOPE - C_ROPE)))
            k_all, v_all = _mla_key(kv_all, kpe_slab, w["c_kb"][j], w["c_vb"][j], w["c_kg"][j],
                                    bd128, cos_k, sin_k)
            u = _attention("mla", qb, k_all, v_all, zs, (), past=past, tk_valid=tk_valid)
        x = _out_proj(x, u, w["out"][i], gate)

    stack = lambda rows: tuple(jnp.stack(r) for r in zip(*rows))
    return x, stack(new_a), stack(new_b), stack(new_c)


def kernel(x_prompt, x_sample, cache_a_k, cache_a_v, cache_a_logf, cache_b_k, cache_b_v, cache_c_kv, cache_c_kpe, c_prompt, c_sample, norm_g, ada_w, ada_b, out_w, a_in_w, a_f_b, a_q_g, a_k_g, b_in_w, b_q_g, b_k_g, b_lam_q1, b_lam_k1, b_lam_q2, b_lam_k2, b_sub_g, c_in_w, c_qa_g, c_kva_g, c_qb_w, c_kvb_w, c_q_g, c_k_g):
    p = dict(norm_g=norm_g, out_w=out_w, a_in_w=a_in_w, a_f_b=a_f_b, a_q_g=a_q_g, a_k_g=a_k_g,
             b_in_w=b_in_w, b_q_g=b_q_g, b_k_g=b_k_g, b_lam_q1=b_lam_q1, b_lam_k1=b_lam_k1,
             b_lam_q2=b_lam_q2, b_lam_k2=b_lam_k2, b_sub_g=b_sub_g, c_in_w=c_in_w,
             c_qa_g=c_qa_g, c_kva_g=c_kva_g, c_qb_w=c_qb_w, c_kvb_w=c_kvb_w, c_q_g=c_q_g,
             c_k_g=c_k_g)
    w = _prep_weights(p)
    bp = x_prompt.shape[0]
    mods = _ada(jnp.concatenate([c_prompt, c_sample], axis=0), ada_w, ada_b)
    caches = (cache_a_k, cache_a_v, cache_a_logf, cache_b_k, cache_b_v, cache_c_kv, cache_c_kpe)
    y_p, (a_k_p, a_v_p, a_lf_p), (b_k_p, b_v_p), (c_kv_p, c_kpe_p) = _run_trunk(
        x_prompt, mods[:, :bp], None, p, w, 0)
    y_s, (a_k_s, a_v_s, a_lf_s), (b_k_s, b_v_s), (c_kv_s, c_kpe_s) = _run_trunk(
        x_sample, mods[:, bp:], caches, p, w, cache_a_k.shape[2])
    return (y_p, y_s, a_k_p, a_v_p, a_lf_p, b_k_p, b_v_p, c_kv_p, c_kpe_p,
            a_k_s, a_v_s, a_lf_s, b_k_s, b_v_s, c_kv_s, c_kpe_s)
```

```python
import functools
import math

import numpy as np
import jax
import jax.numpy as jnp
from jax import lax
from jax.experimental import pallas as pl
from jax.experimental.pallas import tpu as pltpu

F32 = jnp.float32
BF16 = jnp.bfloat16

CHUNK = 64
CHUNK_SHIFT = 6
A_HEAD_DIM = 64
B_HEAD_DIM = 64
C_HEADS = 16
C_NOPE = 64
C_ROPE = 32
C_VDIM = 64
ROPE_THETA = 10000.0
EPS = 1e-6
NEG_INF = -1e30
LOG2E = 1.4426950408889634

LANES = 128
MXU_DIM = 256
V7X_VMEM_BYTES = 64 * 1024 * 1024
VMEM_LIMIT = V7X_VMEM_BYTES * 7 // 8

ROW_TILE = 512
ATTN_TILE = 512
DEC_KEY_TILE = 128
CUMSUM_TILE = 512


def _cparams(*sem):
    return pltpu.CompilerParams(dimension_semantics=sem, vmem_limit_bytes=VMEM_LIMIT)


def _dot(a, b):
    return jnp.dot(a, b, preferred_element_type=F32)


def _split_bf16(x):
    hi = x.astype(BF16)
    lo = (x - hi.astype(F32)).astype(BF16)
    return hi, lo


def _group_sumsq(x, bd):
    outs = []
    for c in range(x.shape[1] // MXU_DIM):
        xc = x[:, c * MXU_DIM:(c + 1) * MXU_DIM]
        hi, lo = _split_bf16(xc * xc)
        outs.append(_dot(hi, bd) + _dot(lo, bd))
    return outs[0] if len(outs) == 1 else jnp.concatenate(outs, axis=1)


def _modulated_norm(x, g, scale, shift):
    y = x * lax.rsqrt(jnp.mean(x * x, axis=-1, keepdims=True) + EPS) * g
    return y * (1.0 + scale) + shift


def _row_norm(x, g):
    return x * lax.rsqrt(jnp.mean(x * x, axis=-1, keepdims=True) + EPS) * g


def _swap_halves(x, half):
    lane = lax.broadcasted_iota(jnp.int32, (1, LANES), 1)
    first = (lane & (2 * half - 1)) < half
    return jnp.where(first, pltpu.roll(x, LANES - half, 1), pltpu.roll(x, half, 1))


def _rope_lanes(x, cos, sin, half):
    outs = []
    for c in range(x.shape[1] // LANES):
        xc = x[:, c * LANES:(c + 1) * LANES]
        outs.append(xc * cos + _swap_halves(xc, half) * sin)
    return outs[0] if len(outs) == 1 else jnp.concatenate(outs, axis=1)


def _silu(z):
    return z * jax.nn.sigmoid(z)


def _log_sigmoid(x):
    return jnp.minimum(x, 0.0) - jnp.log1p(jnp.exp(-jnp.abs(x)))


def _ada_kernel(c_ref, w_ref, b_ref, o_ref):
    c = c_ref[...]
    c_hi, c_lo = _split_bf16(_silu(c))
    w_hi, w_lo = _split_bf16(w_ref[...])
    o_ref[...] = _dot(c_hi, w_hi) + _dot(c_hi, w_lo) + _dot(c_lo, w_hi) + b_ref[...]


def _ada(c_all, ada_w, ada_b):
    depth, d, n3 = ada_w.shape
    rows = c_all.shape[0]
    tn = d
    return pl.pallas_call(
        _ada_kernel,
        grid=(depth, n3 // tn),
        in_specs=[
            pl.BlockSpec((rows, d), lambda l, j: (0, 0)),
            pl.BlockSpec((None, d, tn), lambda l, j: (l, 0, j)),
            pl.BlockSpec((None, 1, tn), lambda l, j: (l, 0, j)),
        ],
        out_specs=pl.BlockSpec((None, rows, tn), lambda l, j: (l, 0, j)),
        out_shape=jax.ShapeDtypeStruct((depth, rows, n3), F32),
        compiler_params=_cparams("arbitrary", "arbitrary"),
        name="ada_modulation",
    )(c_all, ada_w, ada_b.reshape(depth, 1, n3))


def _row_tile(t):
    sublane_pack = 16
    for tm in range(min(ROW_TILE, t), 0, -1):
        if t % tm == 0 and (tm % sublane_pack == 0 or tm == t):
            return tm
    raise ValueError(f"no row tile for {t} rows")


def _row_spec(tm, n):
    return pl.BlockSpec((None, tm, n), lambda b, i: (b, i, 0))


def _batch_spec(n):
    return pl.BlockSpec((None, 1, n), lambda b, i: (b, 0, 0))


def _const_spec(shape):
    return pl.BlockSpec(shape, lambda b, i: (0,) * len(shape))


def _fox_proj_kernel(x_ref, g_ref, sc_ref, sh_ref, w_ref, fb_ref, qg_ref, kg_ref, bd_ref,
                     k_out, v_out, lf_out, qb_out, kb_out, vb_out, zs_out, *, qscale):
    d = x_ref.shape[1]
    h = _modulated_norm(x_ref[...], g_ref[...], sc_ref[...], sh_ref[...]).astype(BF16)
    bd = bd_ref[...]
    inv_dim = 1.0 / A_HEAD_DIM

    q = _dot(h, w_ref[:, 0:d])
    q = q * lax.rsqrt(_group_sumsq(q, bd) * inv_dim + EPS) * qg_ref[...]
    qb_out[...] = (q * qscale).astype(BF16)

    k = _dot(h, w_ref[:, d:2 * d])
    k = k * lax.rsqrt(_group_sumsq(k, bd) * inv_dim + EPS) * kg_ref[...]
    k_out[...] = k
    kb_out[...] = k.astype(BF16)

    v = _dot(h, w_ref[:, 2 * d:3 * d])
    v_out[...] = v
    vb_out[...] = v.astype(BF16)

    zs_out[...] = _silu(_dot(h, w_ref[:, 3 * d:4 * d])).astype(BF16)

    f = _dot(h, w_ref[:, 4 * d:4 * d + LANES]) + fb_ref[...]
    lf_out[...] = _log_sigmoid(f)


def _fox_proj(x, g, scale, shift, w, fb, qg, kg, bd, qscale):
    b, t, d = x.shape
    tm = _row_tile(t)
    f32_out = jax.ShapeDtypeStruct((b, t, d), F32)
    bf_out = jax.ShapeDtypeStruct((b, t, d), BF16)
    return pl.pallas_call(
        functools.partial(_fox_proj_kernel, qscale=qscale),
        grid=(b, t // tm),
        in_specs=[_row_spec(tm, d), _const_spec((1, d)), _batch_spec(d), _batch_spec(d),
                  _const_spec(w.shape), _const_spec((1, LANES)), _const_spec((1, d)),
                  _const_spec((1, d)), _const_spec((MXU_DIM, MXU_DIM))],
        out_specs=[_row_spec(tm, d), _row_spec(tm, d), _row_spec(tm, LANES),
                   _row_spec(tm, d), _row_spec(tm, d), _row_spec(tm, d), _row_spec(tm, d)],
        out_shape=[f32_out, f32_out, jax.ShapeDtypeStruct((b, t, LANES), F32),
                   bf_out, bf_out, bf_out, bf_out],
        compiler_params=_cparams("parallel", "arbitrary"),
        name="fox_in_proj",
    )(x, g, scale, shift, w, fb, qg, kg, bd)


def _diff_proj_kernel(x_ref, g_ref, sc_ref, sh_ref, w_ref, qg_ref, kg_ref, bd_ref, cos_ref, sin_ref,
                      k_out, v_out, qb_out, kb_out, vb_out, zs_out, *, qscale):
    d = x_ref.shape[1]
    h = _modulated_norm(x_ref[...], g_ref[...], sc_ref[...], sh_ref[...]).astype(BF16)
    bd = bd_ref[...]
    cos = cos_ref[...]
    sin = sin_ref[...]
    inv_dim = 1.0 / B_HEAD_DIM
    half = B_HEAD_DIM // 2

    q = _dot(h, w_ref[:, 0:d])
    q = q * lax.rsqrt(_group_sumsq(q, bd) * inv_dim + EPS) * qg_ref[...]
    qb_out[...] = (_rope_lanes(q, cos, sin, half) * qscale).astype(BF16)

    k = _dot(h, w_ref[:, d:2 * d])
    k = k * lax.rsqrt(_group_sumsq(k, bd) * inv_dim + EPS) * kg_ref[...]
    k = _rope_lanes(k, cos, sin, half)
    k_out[...] = k
    kb_out[...] = k.astype(BF16)

    v = _dot(h, w_ref[:, 2 * d:3 * d])
    v_out[...] = v
    vb_out[...] = v.astype(BF16)

    zs_out[...] = _silu(_dot(h, w_ref[:, 3 * d:4 * d])).astype(BF16)


def _diff_proj(x, g, scale, shift, w, qg, kg, bd, cos, sin, qscale):
    b, t, d = x.shape
    tm = _row_tile(t)
    f32_out = jax.ShapeDtypeStruct((b, t, d), F32)
    bf_out = jax.ShapeDtypeStruct((b, t, d), BF16)
    table_spec = pl.BlockSpec((tm, LANES), lambda b_, i: (i, 0))
    return pl.pallas_call(
        functools.partial(_diff_proj_kernel, qscale=qscale),
        grid=(b, t // tm),
        in_specs=[_row_spec(tm, d), _const_spec((1, d)), _batch_spec(d), _batch_spec(d),
                  _const_spec(w.shape), _const_spec((1, d)), _const_spec((1, d)),
                  _const_spec((MXU_DIM, MXU_DIM)), table_spec, table_spec],
        out_specs=[_row_spec(tm, d)] * 6,
        out_shape=[f32_out, f32_out, bf_out, bf_out, bf_out, bf_out],
        compiler_params=_cparams("parallel", "arbitrary"),
        name="diff_in_proj",
    )(x, g, scale, shift, w, qg, kg, bd, cos, sin)


def _mla_proj_kernel(x_ref, g_ref, sc_ref, sh_ref, w_ref, qag_ref, kvag_ref, qbw_ref, qg_ref,
                     bd_ref, cos_ref, sin_ref,
                     kv_out, kpe_out, qb_out, zs_out, *, qscale, q_lora, kv_lora):
    d = x_ref.shape[1]
    h = _modulated_norm(x_ref[...], g_ref[...], sc_ref[...], sh_ref[...]).astype(BF16)
    n_a = q_lora + kv_lora + LANES

    a = _dot(h, w_ref[:, 0:n_a])
    qa = _row_norm(a[:, 0:q_lora], qag_ref[...]).astype(BF16)
    kv_out[...] = _row_norm(a[:, q_lora:q_lora + kv_lora], kvag_ref[...])
    kpe_out[...] = a[:, q_lora + kv_lora:q_lora + kv_lora + kpe_out.shape[1]]

    zs_out[...] = _silu(_dot(h, w_ref[:, n_a:n_a + d])).astype(BF16)

    q = _dot(qa, qbw_ref[...])
    inv_dim = 1.0 / (C_NOPE + C_ROPE)
    q = q * lax.rsqrt(_group_sumsq(q, bd_ref[...]) * inv_dim + EPS) * qg_ref[...]
    qb_out[...] = (_rope_lanes(q, cos_ref[...], sin_ref[...], C_ROPE // 2) * qscale).astype(BF16)


def _mla_proj(x, g, scale, shift, w, qag, kvag, qbw, qg, bd, cos, sin, qscale):
    b, t, d = x.shape
    tm = _row_tile(t)
    q_lora = qag.shape[1]
    kv_lora = kvag.shape[1]
    nslab = qbw.shape[1]
    table_spec = pl.BlockSpec((tm, LANES), lambda b_, i: (i, 0))
    return pl.pallas_call(
        functools.partial(_mla_proj_kernel, qscale=qscale, q_lora=q_lora, kv_lora=kv_lora),
        grid=(b, t // tm),
        in_specs=[_row_spec(tm, d), _const_spec((1, d)), _batch_spec(d), _batch_spec(d),
                  _const_spec(w.shape), _const_spec((1, q_lora)), _const_spec((1, kv_lora)),
                  _const_spec(qbw.shape), _const_spec((1, nslab)),
                  _const_spec((MXU_DIM, MXU_DIM)), table_spec, table_spec],
        out_specs=[_row_spec(tm, kv_lora), _row_spec(tm, C_ROPE), _row_spec(tm, nslab),
                   _row_spec(tm, d)],
        out_shape=[jax.ShapeDtypeStruct((b, t, kv_lora), F32),
                   jax.ShapeDtypeStruct((b, t, C_ROPE), F32),
                   jax.ShapeDtypeStruct((b, t, nslab), BF16),
                   jax.ShapeDtypeStruct((b, t, d), BF16)],
        compiler_params=_cparams("parallel", "arbitrary"),
        name="mla_in_proj",
    )(x, g, scale, shift, w, qag, kvag, qbw, qg, bd, cos, sin)


def _mla_key_kernel(kv_ref, kpe_ref, wk_ref, wv_ref, kg_ref, bd_ref, cos_ref, sin_ref,
                    kb_out, vb_out):
    kv = kv_ref[...].astype(BF16)
    vb_out[...] = _dot(kv, wv_ref[...]).astype(BF16)
    kn = _dot(kv, wk_ref[...])
    kpe = kpe_ref[...]
    k = jnp.concatenate([kn[:, c * LANES:(c + 1) * LANES] + kpe
                         for c in range(kn.shape[1] // LANES)], axis=1)
    inv_dim = 1.0 / (C_NOPE + C_ROPE)
    k = k * lax.rsqrt(_group_sumsq(k, bd_ref[...]) * inv_dim + EPS) * kg_ref[...]
    kb_out[...] = _rope_lanes(k, cos_ref[...], sin_ref[...], C_ROPE // 2).astype(BF16)


def _mla_key(kv_all, kpe_all, wk, wv, kg, bd, cos, sin):
    b, tk, kv_lora = kv_all.shape
    tm = _row_tile(tk)
    nslab = wk.shape[1]
    nv = wv.shape[1]
    table_spec = pl.BlockSpec((tm, LANES), lambda b_, i: (i, 0))
    return pl.pallas_call(
        _mla_key_kernel,
        grid=(b, tk // tm),
        in_specs=[_row_spec(tm, kv_lora), _row_spec(tm, LANES), _const_spec(wk.shape),
                  _const_spec(wv.shape), _const_spec((1, nslab)),
                  _const_spec((MXU_DIM, MXU_DIM)), table_spec, table_spec],
        out_specs=[_row_spec(tm, nslab), _row_spec(tm, nv)],
        out_shape=[jax.ShapeDtypeStruct((b, tk, nslab), BF16),
                   jax.ShapeDtypeStruct((b, tk, nv), BF16)],
        compiler_params=_cparams("parallel", "arbitrary"),
        name="mla_key_proj",
    )(kv_all, kpe_all, wk, wv, kg, bd, cos, sin)


def _out_proj_kernel(x_ref, u_ref, w_ref, gate_ref, o_ref):
    o_ref[...] = x_ref[...] + gate_ref[...] * _dot(u_ref[...], w_ref[...])


def _out_proj(x, u, w, gate):
    b, t, d = x.shape
    tm = _row_tile(t)
    return pl.pallas_call(
        _out_proj_kernel,
        grid=(b, t // tm),
        in_specs=[_row_spec(tm, d), _row_spec(tm, d), _const_spec(w.shape), _batch_spec(d)],
        out_specs=_row_spec(tm, d),
        out_shape=jax.ShapeDtypeStruct((b, t, d), F32),
        compiler_params=_cparams("parallel", "arbitrary"),
        name="out_proj_residual",
    )(x, u, w, gate)


def _split3_bf16(x):
    hi = x.astype(BF16)
    r = x - hi.astype(F32)
    mid = r.astype(BF16)
    lo = (r - mid.astype(F32)).astype(BF16)
    return hi, mid, lo


def _cumsum_kernel(x_ref, tri_ref, place_ref, cum_out, ext_out, carry_ref):
    @pl.when(pl.program_id(1) == 0)
    def _():
        carry_ref[...] = jnp.zeros_like(carry_ref)

    tri = tri_ref[...]
    c = carry_ref[...]
    for part in _split3_bf16(x_ref[...]):
        c = c + _dot(tri, part)
    cum_out[...] = c
    carry_ref[...] = c[c.shape[0] - 1:, :]
    ext = jnp.zeros(ext_out.shape, F32)
    for i, part in enumerate(_split3_bf16(c * LOG2E)):
        ext = ext + _dot(part, place_ref[i])
    ext_out[...] = ext.astype(BF16)


def _cumsum_rows(x, heads):
    b, t, n = x.shape
    tc = CUMSUM_TILE if t % CUMSUM_TILE == 0 else t
    tri = jnp.asarray(np.tril(np.ones((tc, tc), np.float32)), BF16)
    place = np.zeros((3, n, n), np.float32)
    for part in range(3):
        place[part, np.arange(heads), 3 * np.arange(heads) + part] = 1.0
    return pl.pallas_call(
        _cumsum_kernel,
        grid=(b, t // tc),
        in_specs=[_row_spec(tc, n), _const_spec((tc, tc)), _const_spec((3, n, n))],
        out_specs=[_row_spec(tc, n), _row_spec(tc, n)],
        out_shape=[jax.ShapeDtypeStruct((b, t, n), F32), jax.ShapeDtypeStruct((b, t, n), BF16)],
        scratch_shapes=[pltpu.VMEM((1, n), F32)],
        compiler_params=_cparams("parallel", "arbitrary"),
        name="logf_cumsum",
    )(x, tri, jnp.asarray(place, BF16))


def _attn_kernel(*refs, mode, tq, tk, nkb, past, tk_valid, lam_init):
    if mode == "fox":
        q_ref, k_ref, v_ref, zs_ref, kx_ref, qc_ref, o_ref, vt_ref, m_ref, l_ref, acc_ref = refs
    elif mode == "diff":
        q_ref, k_ref, v_ref, zs_ref, lam_ref, subg_ref, o_ref, vt_ref, m_ref, l_ref, acc_ref = refs
    else:
        q_ref, k_ref, v_ref, zs_ref, o_ref, vt_ref, m_ref, l_ref, acc_ref = refs

    @pl.when(pl.program_id(2) == 0)
    def _():
        for c in range(nkb):
            vt_ref[c] = v_ref[c * tk:(c + 1) * tk, :].astype(F32).T.astype(BF16)

    q0 = pl.program_id(2) * tq
    row = lax.broadcasted_iota(jnp.int32, (LANES, 1), 0)
    low = row < (LANES // 2)

    qt = q_ref[...].astype(F32).T
    if mode == "mla":
        rhs = (qt[0:LANES].astype(BF16), qt[LANES:2 * LANES].astype(BF16))
    else:
        rhs = (jnp.where(low, qt, 0.0).astype(BF16), jnp.where(low, 0.0, qt).astype(BF16))

    if mode == "fox":
        pair = pl.program_id(1)
        picks = []
        for h in range(2):
            first = 3 * (2 * pair + h)
            sel = jnp.logical_and(row >= first, row < first + 3)
            picks.append(jnp.broadcast_to(jnp.where(sel, -1.0, 0.0), (LANES, tq)).astype(BF16))
        rhs = tuple(jnp.concatenate([rhs[h], picks[h]], axis=0) for h in range(2))
        qcl = tuple(qc_ref[h:h + 1, :] * LOG2E for h in range(2))
    else:
        qcl = (None, None)

    m_ref[...] = jnp.full(m_ref.shape, NEG_INF, F32)
    l_ref[...] = jnp.zeros(l_ref.shape, F32)
    acc_ref[...] = jnp.zeros(acc_ref.shape, F32)

    q_pos = past + q0 + lax.broadcasted_iota(jnp.int32, (1, tq), 1)

    def step(j, masked):
        k0 = pl.multiple_of(j * tk, tk)
        vt = vt_ref[j]
        if mode == "mla":
            lhs = (k_ref[pl.ds(k0, tk), 0:LANES], k_ref[pl.ds(k0, tk), LANES:2 * LANES])
        elif mode == "fox":
            kk = jnp.concatenate([k_ref[pl.ds(k0, tk), :], kx_ref[pl.ds(k0, tk), :]], axis=1)
            lhs = (kk, kk)
        else:
            kk = k_ref[pl.ds(k0, tk), :]
            lhs = (kk, kk)
        if masked:
            k_pos = k0 + lax.broadcasted_iota(jnp.int32, (tk, 1), 0)
            if mode == "fox":
                mask = k_pos <= q_pos
            else:
                mask = (k_pos >> CHUNK_SHIFT) <= (q_pos >> CHUNK_SHIFT)
                if tk_valid < nkb * tk:
                    mask = jnp.logical_and(mask, k_pos < tk_valid)
        for h in range(2):
            s = _dot(lhs[h], rhs[h])
            if masked:
                s = jnp.where(mask, s, NEG_INF)
            m_old = m_ref[h]
            col_max = jnp.max(s, axis=0, keepdims=True)
            if mode == "fox":
                m_new = jnp.maximum(m_old, col_max + qcl[h])
                p = jnp.exp2(s - (m_new - qcl[h]))
            else:
                m_new = jnp.maximum(m_old, col_max)
                p = jnp.exp2(s - m_new)
            alpha = jnp.exp2(m_old - m_new)
            l_ref[h] = alpha * l_ref[h] + jnp.sum(p, axis=0, keepdims=True)
            acc_ref[h] = alpha * acc_ref[h] + _dot(vt, p.astype(BF16))
            m_ref[h] = m_new

    if mode == "fox":
        n_full = (past + q0 + 1) // tk
        j_last = (past + q0 + tq - 1) // tk
    else:
        n_full = ((past + q0) // CHUNK + 1) * CHUNK // tk
        j_last = ((past + q0 + tq - 1) // CHUNK * CHUNK + CHUNK - 1) // tk
    n_full = jnp.minimum(n_full, tk_valid // tk)
    j_last = jnp.minimum(j_last, nkb - 1)

    def full_body(j, carry):
        step(j, False)
        return carry

    def masked_body(j, carry):
        step(j, True)
        return carry

    lax.fori_loop(0, n_full, full_body, 0)
    lax.fori_loop(n_full, j_last + 1, masked_body, 0)

    o0 = acc_ref[0] / l_ref[0]
    o1 = acc_ref[1] / l_ref[1]
    if mode == "diff":
        lv = lam_ref[...]
        lam = (jnp.exp(jnp.sum(lv[0:1] * lv[1:2], axis=-1, keepdims=True))
               - jnp.exp(jnp.sum(lv[2:3] * lv[3:4], axis=-1, keepdims=True)) + lam_init)
        ot = o0 - lam * o1
        ot = ot * lax.rsqrt(jnp.mean(ot * ot, axis=0, keepdims=True) + EPS)
        ot = ot * subg_ref[...] * (1.0 - lam_init)
    else:
        ot = jnp.where(low, o0, o1)
    o_ref[...] = (ot.T * zs_ref[...].astype(F32)).astype(BF16)


def _attention(mode, q, k_all, v_all, zs, extras, *, past, tk_valid, lam_init=0.0):
    b, t, d = zs.shape
    tk_total = k_all.shape[1]
    if past == 0:
        tq = min(ATTN_TILE, t)
        tk = tq
        t_run = t
    else:
        tq = LANES
        tk = DEC_KEY_TILE
        t_run = -(-t // tq) * tq
        q = _pad_rows(q, t_run)
        zs = _pad_rows(zs, t_run)
    nkb = tk_total // tk
    groups = d // LANES
    qw = q.shape[2] // groups

    in_specs = [
        pl.BlockSpec((None, tq, qw), lambda b_, g, i: (b_, i, g)),
        pl.BlockSpec((None, tk_total, qw), lambda b_, g, i: (b_, 0, g)),
        pl.BlockSpec((None, tk_total, LANES), lambda b_, g, i: (b_, 0, g)),
        pl.BlockSpec((None, tq, LANES), lambda b_, g, i: (b_, i, g)),
    ]
    if mode == "fox":
        kx, qc = extras
        extras = (kx, _pad_rows(qc, t_run, axis=3))
        in_specs += [
            pl.BlockSpec((None, tk_total, LANES), lambda b_, g, i: (b_, 0, 0)),
            pl.BlockSpec((None, None, 2, tq), lambda b_, g, i: (b_, g, 0, i)),
        ]
    elif mode == "diff":
        lam_vecs, sub_g = extras
        in_specs += [
            pl.BlockSpec(lam_vecs.shape, lambda b_, g, i: (0, 0)),
            pl.BlockSpec(sub_g.shape, lambda b_, g, i: (0, 0)),
        ]
    kern = functools.partial(_attn_kernel, mode=mode, tq=tq, tk=tk, nkb=nkb, past=past,
                             tk_valid=tk_valid, lam_init=lam_init)
    u = pl.pallas_call(
        kern,
        grid=(b, groups, t_run // tq),
        in_specs=in_specs,
        out_specs=pl.BlockSpec((None, tq, LANES), lambda b_, g, i: (b_, i, g)),
        out_shape=jax.ShapeDtypeStruct((b, t_run, d), BF16),
        scratch_shapes=[pltpu.VMEM((nkb, LANES, tk), BF16),
                        pltpu.VMEM((2, 1, tq), F32), pltpu.VMEM((2, 1, tq), F32),
                        pltpu.VMEM((2, LANES, tq), F32)],
        compiler_params=_cparams("parallel", "parallel", "arbitrary"),
        name=mode + "_attention",
    )(q, k_all, v_all, zs, *extras)
    return u[:, :t]


def _group_matrix(group):
    idx = np.arange(MXU_DIM) // group
    return jnp.asarray((idx[:, None] == idx[None, :]).astype(np.float32), BF16)


def _rope_tables(pos, half, lead, trail):
    inv = ROPE_THETA ** (-jnp.arange(half, dtype=F32) / half)
    ang = pos.astype(F32)[:, None] * inv
    cos = jnp.cos(ang)
    sin = jnp.sin(ang)
    reps = (LANES - lead - trail) // (2 * half)
    n = pos.shape[0]
    cos_t = jnp.concatenate([jnp.ones((n, lead), F32)] + [cos, cos] * reps
                            + [jnp.ones((n, trail), F32)], axis=1)
    sin_t = jnp.concatenate([jnp.zeros((n, lead), F32)] + [-sin, sin] * reps
                            + [jnp.zeros((n, trail), F32)], axis=1)
    return cos_t, sin_t


def _pad_rows(x, rows, axis=1):
    if x.shape[axis] == rows:
        return x
    pad = [(0, 0)] * x.ndim
    pad[axis] = (0, rows - x.shape[axis])
    return jnp.pad(x, pad)


def _key_rows(past, t):
    if past == 0:
        return t
    return -(-(past + t) // DEC_KEY_TILE) * DEC_KEY_TILE


def _prep_weights(p):
    d = p["norm_g"].shape[1]
    w = {}
    a_in = p["a_in_w"]
    heads_a = d // A_HEAD_DIM
    w["a_in"] = jnp.concatenate(
        [a_in[:, :, :3 * d], a_in[:, :, 3 * d + heads_a:],
         jnp.pad(a_in[:, :, 3 * d:3 * d + heads_a], ((0, 0), (0, 0), (0, LANES - heads_a)))],
        axis=2).astype(BF16)
    w["a_fb"] = jnp.pad(p["a_f_b"], ((0, 0), (0, LANES - heads_a)))[:, None, :]
    w["a_qg"] = jnp.tile(p["a_q_g"], (1, heads_a))[:, None, :]
    w["a_kg"] = jnp.tile(p["a_k_g"], (1, heads_a))[:, None, :]

    heads_b = d // (2 * B_HEAD_DIM)
    w["b_in"] = p["b_in_w"].astype(BF16)
    nb = p["b_q_g"].shape[0]
    w["b_qg"] = jnp.tile(p["b_q_g"].reshape(nb, 2 * B_HEAD_DIM), (1, heads_b))[:, None, :]
    w["b_kg"] = jnp.tile(p["b_k_g"].reshape(nb, 2 * B_HEAD_DIM), (1, heads_b))[:, None, :]
    w["b_lam"] = jnp.stack([p["b_lam_q1"], p["b_lam_k1"], p["b_lam_q2"], p["b_lam_k2"]], axis=1)
    w["b_subg"] = p["b_sub_g"][:, :, None]

    c_in = p["c_in_w"]
    q_lora = p["c_qa_g"].shape[1]
    kv_lora = p["c_kva_g"].shape[1]
    n0 = q_lora + kv_lora
    w["c_in"] = jnp.concatenate(
        [c_in[:, :, :n0],
         jnp.pad(c_in[:, :, n0:n0 + C_ROPE], ((0, 0), (0, 0), (0, LANES - C_ROPE))),
         c_in[:, :, n0 + C_ROPE:]], axis=2).astype(BF16)
    w["c_qag"] = p["c_qa_g"][:, None, :]
    w["c_kvag"] = p["c_kva_g"][:, None, :]
    nc = c_in.shape[0]
    hd = C_NOPE + C_ROPE
    qb = p["c_qb_w"].reshape(nc, q_lora, C_HEADS, hd)
    w["c_qb"] = jnp.pad(qb, ((0, 0), (0, 0), (0, 0), (0, LANES - hd))).reshape(
        nc, q_lora, C_HEADS * LANES).astype(BF16)
    kvb = p["c_kvb_w"].reshape(nc, kv_lora, C_HEADS, C_NOPE + C_VDIM)
    w["c_kb"] = jnp.pad(kvb[..., :C_NOPE], ((0, 0), (0, 0), (0, 0), (0, LANES - C_NOPE))).reshape(
        nc, kv_lora, C_HEADS * LANES).astype(BF16)
    w["c_vb"] = kvb[..., C_NOPE:].reshape(nc, kv_lora, C_HEADS * C_VDIM).astype(BF16)
    slab_gain = lambda g: jnp.tile(jnp.pad(g, ((0, 0), (0, LANES - hd))), (1, C_HEADS))[:, None, :]
    w["c_qg"] = slab_gain(p["c_q_g"])
    w["c_kg"] = slab_gain(p["c_k_g"])
    w["out"] = p["out_w"].astype(BF16)
    return w


def _run_trunk(x, mods, caches, p, w, past):
    b, t, d = x.shape
    depth = p["norm_g"].shape[0]
    heads_a = d // A_HEAD_DIM
    tk_valid = past + t
    tk_total = _key_rows(past, t)
    q_pos = past + jnp.arange(t)
    k_pos = jnp.arange(tk_total)
    bd64 = _group_matrix(A_HEAD_DIM)
    bd128 = _group_matrix(LANES)
    new_a, new_b, new_c = [], [], []

    def with_past(cache, new_bf):
        if past == 0:
            return new_bf
        old = cache.reshape(b, past, -1).astype(BF16)
        return _pad_rows(jnp.concatenate([old, new_bf], axis=1), tk_total)

    for i in range(depth):
        j = i // 3
        shift = mods[i][:, None, 0:d]
        scale = mods[i][:, None, d:2 * d]
        gate = mods[i][:, None, 2 * d:3 * d]
        g = p["norm_g"][i][None, :]
        if i % 3 == 0:
            qscale = A_HEAD_DIM ** -0.5 * LOG2E
            k, v, logf, qb, kb, vb, zs = _fox_proj(
                x, g, scale, shift, w["a_in"][j], w["a_fb"][j], w["a_qg"][j], w["a_kg"][j],
                bd64, qscale)
            new_a.append((k.reshape(b, t, heads_a, A_HEAD_DIM),
                          v.reshape(b, t, heads_a, A_HEAD_DIM), logf[:, :, :heads_a]))
            if past == 0:
                k_all, v_all, lf_all = kb, vb, logf
            else:
                k_all = with_past(caches[0][j], kb)
                v_all = with_past(caches[1][j], vb)
                lf_old = _pad_rows(caches[2][j], LANES, axis=2)
                lf_all = _pad_rows(jnp.concatenate([lf_old, logf], axis=1), tk_total)
            cum, kx = _cumsum_rows(lf_all, heads_a)
            qc = jnp.transpose(cum[:, past:past + t, :heads_a], (0, 2, 1)).reshape(
                b, heads_a // 2, 2, t)
            u = _attention("fox", qb, k_all, v_all, zs, (kx, qc), past=past, tk_valid=tk_valid)
        elif i % 3 == 1:
            qscale = B_HEAD_DIM ** -0.5 * LOG2E
            cos, sin = _rope_tables(q_pos, B_HEAD_DIM // 2, 0, 0)
            k, v, qb, kb, vb, zs = _diff_proj(
                x, g, scale, shift, w["b_in"][j], w["b_qg"][j], w["b_kg"][j], bd64, cos, sin,
                qscale)
            heads_b = d // (2 * B_HEAD_DIM)
            new_b.append((k.reshape(b, t, heads_b, 2, B_HEAD_DIM),
                          v.reshape(b, t, heads_b, 2 * B_HEAD_DIM)))
            k_all = with_past(None if past == 0 else caches[3][j], kb)
            v_all = with_past(None if past == 0 else caches[4][j], vb)
            lam_init = 0.8 - 0.6 * math.exp(-0.3 * i)
            u = _attention("diff", qb, k_all, v_all, zs, (w["b_lam"][j], w["b_subg"][j]),
                           past=past, tk_valid=tk_valid, lam_init=lam_init)
        else:
            qscale = (C_NOPE + C_ROPE) ** -0.5 * LOG2E
            cos_q, sin_q = _rope_tables(q_pos, C_ROPE // 2, C_NOPE, LANES - C_NOPE - C_ROPE)
            cos_k, sin_k = _rope_tables(k_pos, C_ROPE // 2, C_NOPE, LANES - C_NOPE - C_ROPE)
            kv_lat, kpe, qb, zs = _mla_proj(
                x, g, scale, shift, w["c_in"][j], w["c_qag"][j], w["c_kvag"][j], w["c_qb"][j],
                w["c_qg"][j], bd128, cos_q, sin_q, qscale)
            new_c.append((kv_lat, kpe))
            if past == 0:
                kv_all, kpe_all = kv_lat, kpe
            else:
                kv_all = _pad_rows(jnp.concatenate([caches[5][j], kv_lat], axis=1), tk_total)
                kpe_all = _pad_rows(jnp.concatenate([caches[6][j], kpe], axis=1), tk_total)
            kpe_slab = jnp.pad(kpe_all, ((0, 0), (0, 0), (C_NOPE, LANES - C_NOPE - C_ROPE)))
            k_all, v_all = _mla_key(kv_all, kpe_slab, w["c_kb"][j], w["c_vb"][j], w["c_kg"][j],
                                    bd128, cos_k, sin_k)
            u = _attention("mla", qb, k_all, v_all, zs, (), past=past, tk_valid=tk_valid)
        x = _out_proj(x, u, w["out"][i], gate)

    stack = lambda rows: tuple(jnp.stack(r) for r in zip(*rows))
    return x, stack(new_a), stack(new_b), stack(new_c)


def kernel(x_prompt, x_sample, cache_a_k, cache_a_v, cache_a_logf, cache_b_k, cache_b_v, cache_c_kv, cache_c_kpe, c_prompt, c_sample, norm_g, ada_w, ada_b, out_w, a_in_w, a_f_b, a_q_g, a_k_g, b_in_w, b_q_g, b_k_g, b_lam_q1, b_lam_k1, b_lam_q2, b_lam_k2, b_sub_g, c_in_w, c_qa_g, c_kva_g, c_qb_w, c_kvb_w, c_q_g, c_k_g):
    p = dict(norm_g=norm_g, out_w=out_w, a_in_w=a_in_w, a_f_b=a_f_b, a_q_g=a_q_g, a_k_g=a_k_g,
             b_in_w=b_in_w, b_q_g=b_q_g, b_k_g=b_k_g, b_lam_q1=b_lam_q1, b_lam_k1=b_lam_k1,
             b_lam_q2=b_lam_q2, b_lam_k2=b_lam_k2, b_sub_g=b_sub_g, c_in_w=c_in_w,
             c_qa_g=c_qa_g, c_kva_g=c_kva_g, c_qb_w=c_qb_w, c_kvb_w=c_kvb_w, c_q_g=c_q_g,
             c_k_g=c_k_g)
    w = _prep_weights(p)
    bp = x_prompt.shape[0]
    mods = _ada(jnp.concatenate([c_prompt, c_sample], axis=0), ada_w, ada_b)
    caches = (cache_a_k, cache_a_v, cache_a_logf, cache_b_k, cache_b_v, cache_c_kv, cache_c_kpe)
    y_p, (a_k_p, a_v_p, a_lf_p), (b_k_p, b_v_p), (c_kv_p, c_kpe_p) = _run_trunk(
        x_prompt, mods[:, :bp], None, p, w, 0)
    y_s, (a_k_s, a_v_s, a_lf_s), (b_k_s, b_v_s), (c_kv_s, c_kpe_s) = _run_trunk(
        x_sample, mods[:, bp:], caches, p, w, cache_a_k.shape[2])
    return (y_p, y_s, a_k_p, a_v_p, a_lf_p, b_k_p, b_v_p, c_kv_p, c_kpe_p,
            a_k_s, a_v_s, a_lf_s, b_k_s, b_v_s, c_kv_s, c_kpe_s)
```

```python
import functools
import math

import numpy as np
import jax
import jax.numpy as jnp
from jax import lax
from jax.experimental import pallas as pl
from jax.experimental.pallas import tpu as pltpu

F32 = jnp.float32
BF16 = jnp.bfloat16

CHUNK = 64
CHUNK_SHIFT = 6
A_HEAD_DIM = 64
B_HEAD_DIM = 64
C_HEADS = 16
C_NOPE = 64
C_ROPE = 32
C_VDIM = 64
ROPE_THETA = 10000.0
EPS = 1e-6
NEG_INF = -1e30
LOG2E = 1.4426950408889634

FIXED_OFFSET_MAX_BOUND = 40.0
BOUND_SLACK_REL = 1.02
BOUND_SLACK_ABS = 1.0
FIXED_WIDE = 2

LANES = 128
MXU_DIM = 256
V7X_VMEM_BYTES = 64 * 1024 * 1024
VMEM_LIMIT = V7X_VMEM_BYTES * 7 // 8

ROW_TILE = 512
ATTN_TILE = 512
DEC_KEY_TILE = 128
CUMSUM_TILE = 512


def _cparams(*sem):
    return pltpu.CompilerParams(dimension_semantics=sem, vmem_limit_bytes=VMEM_LIMIT)


def _dot(a, b):
    return jnp.dot(a, b, preferred_element_type=F32)


def _split_bf16(x):
    hi = x.astype(BF16)
    lo = (x - hi.astype(F32)).astype(BF16)
    return hi, lo


def _group_sumsq(x, bd):
    outs = []
    for c in range(x.shape[1] // MXU_DIM):
        xc = x[:, c * MXU_DIM:(c + 1) * MXU_DIM]
        hi, lo = _split_bf16(xc * xc)
        outs.append(_dot(hi, bd) + _dot(lo, bd))
    return outs[0] if len(outs) == 1 else jnp.concatenate(outs, axis=1)


def _modulated_norm(x, g, scale, shift):
    y = x * lax.rsqrt(jnp.mean(x * x, axis=-1, keepdims=True) + EPS) * g
    return y * (1.0 + scale) + shift


def _row_norm(x, g):
    return x * lax.rsqrt(jnp.mean(x * x, axis=-1, keepdims=True) + EPS) * g


def _swap_halves(x, half):
    lane = lax.broadcasted_iota(jnp.int32, (1, LANES), 1)
    first = (lane & (2 * half - 1)) < half
    return jnp.where(first, pltpu.roll(x, LANES - half, 1), pltpu.roll(x, half, 1))


def _rope_lanes(x, cos, sin, half):
    outs = []
    for c in range(x.shape[1] // LANES):
        xc = x[:, c * LANES:(c + 1) * LANES]
        outs.append(xc * cos + _swap_halves(xc, half) * sin)
    return outs[0] if len(outs) == 1 else jnp.concatenate(outs, axis=1)


def _silu(z):
    return z * jax.nn.sigmoid(z)


def _log_sigmoid(x):
    return jnp.minimum(x, 0.0) - jnp.log1p(jnp.exp(-jnp.abs(x)))


def _ada_kernel(c_ref, w_ref, b_ref, o_ref):
    c = c_ref[...]
    c_hi, c_lo = _split_bf16(_silu(c))
    w_hi, w_lo = _split_bf16(w_ref[...])
    o_ref[...] = _dot(c_hi, w_hi) + _dot(c_hi, w_lo) + _dot(c_lo, w_hi) + b_ref[...]


def _ada(c_all, ada_w, ada_b):
    depth, d, n3 = ada_w.shape
    rows = c_all.shape[0]
    tn = d
    return pl.pallas_call(
        _ada_kernel,
        grid=(depth, n3 // tn),
        in_specs=[
            pl.BlockSpec((rows, d), lambda l, j: (0, 0)),
            pl.BlockSpec((None, d, tn), lambda l, j: (l, 0, j)),
            pl.BlockSpec((None, 1, tn), lambda l, j: (l, 0, j)),
        ],
        out_specs=pl.BlockSpec((None, rows, tn), lambda l, j: (l, 0, j)),
        out_shape=jax.ShapeDtypeStruct((depth, rows, n3), F32),
        compiler_params=_cparams("arbitrary", "arbitrary"),
        name="ada_modulation",
    )(c_all, ada_w, ada_b.reshape(depth, 1, n3))


def _row_tile(t):
    sublane_pack = 16
    for tm in range(min(ROW_TILE, t), 0, -1):
        if t % tm == 0 and (tm % sublane_pack == 0 or tm == t):
            return tm
    raise ValueError(f"no row tile for {t} rows")


def _row_spec(tm, n):
    return pl.BlockSpec((None, tm, n), lambda b, i: (b, i, 0))


def _batch_spec(n):
    return pl.BlockSpec((None, 1, n), lambda b, i: (b, 0, 0))


def _const_spec(shape):
    return pl.BlockSpec(shape, lambda b, i: (0,) * len(shape))


def _fox_proj_kernel(x_ref, g_ref, sc_ref, sh_ref, w_ref, fb_ref, qg_ref, kg_ref, bd_ref,
                     k_out, v_out, lf_out, qb_out, kb_out, vb_out, zs_out, *, qscale):
    d = x_ref.shape[1]
    h = _modulated_norm(x_ref[...], g_ref[...], sc_ref[...], sh_ref[...]).astype(BF16)
    bd = bd_ref[...]
    inv_dim = 1.0 / A_HEAD_DIM

    q = _dot(h, w_ref[:, 0:d])
    q = q * lax.rsqrt(_group_sumsq(q, bd) * inv_dim + EPS) * qg_ref[...]
    qb_out[...] = (q * qscale).astype(BF16)

    k = _dot(h, w_ref[:, d:2 * d])
    k = k * lax.rsqrt(_group_sumsq(k, bd) * inv_dim + EPS) * kg_ref[...]
    k_out[...] = k
    kb_out[...] = k.astype(BF16)

    v = _dot(h, w_ref[:, 2 * d:3 * d])
    v_out[...] = v
    vb_out[...] = v.astype(BF16)

    zs_out[...] = _silu(_dot(h, w_ref[:, 3 * d:4 * d])).astype(BF16)

    f = _dot(h, w_ref[:, 4 * d:4 * d + LANES]) + fb_ref[...]
    lf_out[...] = _log_sigmoid(f)


def _fox_proj(x, g, scale, shift, w, fb, qg, kg, bd, qscale):
    b, t, d = x.shape
    tm = _row_tile(t)
    f32_out = jax.ShapeDtypeStruct((b, t, d), F32)
    bf_out = jax.ShapeDtypeStruct((b, t, d), BF16)
    return pl.pallas_call(
        functools.partial(_fox_proj_kernel, qscale=qscale),
        grid=(b, t // tm),
        in_specs=[_row_spec(tm, d), _const_spec((1, d)), _batch_spec(d), _batch_spec(d),
                  _const_spec(w.shape), _const_spec((1, LANES)), _const_spec((1, d)),
                  _const_spec((1, d)), _const_spec((MXU_DIM, MXU_DIM))],
        out_specs=[_row_spec(tm, d), _row_spec(tm, d), _row_spec(tm, LANES),
                   _row_spec(tm, d), _row_spec(tm, d), _row_spec(tm, d), _row_spec(tm, d)],
        out_shape=[f32_out, f32_out, jax.ShapeDtypeStruct((b, t, LANES), F32),
                   bf_out, bf_out, bf_out, bf_out],
        compiler_params=_cparams("parallel", "arbitrary"),
        name="fox_in_proj",
    )(x, g, scale, shift, w, fb, qg, kg, bd)


def _diff_proj_kernel(x_ref, g_ref, sc_ref, sh_ref, w_ref, qg_ref, kg_ref, bd_ref, cos_ref, sin_ref,
                      k_out, v_out, qb_out, kb_out, vb_out, zs_out, *, qscale):
    d = x_ref.shape[1]
    h = _modulated_norm(x_ref[...], g_ref[...], sc_ref[...], sh_ref[...]).astype(BF16)
    bd = bd_ref[...]
    cos = cos_ref[...]
    sin = sin_ref[...]
    inv_dim = 1.0 / B_HEAD_DIM
    half = B_HEAD_DIM // 2

    q = _dot(h, w_ref[:, 0:d])
    q = q * lax.rsqrt(_group_sumsq(q, bd) * inv_dim + EPS) * qg_ref[...]
    qb_out[...] = (_rope_lanes(q, cos, sin, half) * qscale).astype(BF16)

    k = _dot(h, w_ref[:, d:2 * d])
    k = k * lax.rsqrt(_group_sumsq(k, bd) * inv_dim + EPS) * kg_ref[...]
    k = _rope_lanes(k, cos, sin, half)
    k_out[...] = k
    kb_out[...] = k.astype(BF16)

    v = _dot(h, w_ref[:, 2 * d:3 * d])
    v_out[...] = v
    vb_out[...] = v.astype(BF16)

    zs_out[...] = _silu(_dot(h, w_ref[:, 3 * d:4 * d])).astype(BF16)


def _diff_proj(x, g, scale, shift, w, qg, kg, bd, cos, sin, qscale):
    b, t, d = x.shape
    tm = _row_tile(t)
    f32_out = jax.ShapeDtypeStruct((b, t, d), F32)
    bf_out = jax.ShapeDtypeStruct((b, t, d), BF16)
    table_spec = pl.BlockSpec((tm, LANES), lambda b_, i: (i, 0))
    return pl.pallas_call(
        functools.partial(_diff_proj_kernel, qscale=qscale),
        grid=(b, t // tm),
        in_specs=[_row_spec(tm, d), _const_spec((1, d)), _batch_spec(d), _batch_spec(d),
                  _const_spec(w.shape), _const_spec((1, d)), _const_spec((1, d)),
                  _const_spec((MXU_DIM, MXU_DIM)), table_spec, table_spec],
        out_specs=[_row_spec(tm, d)] * 6,
        out_shape=[f32_out, f32_out, bf_out, bf_out, bf_out, bf_out],
        compiler_params=_cparams("parallel", "arbitrary"),
        name="diff_in_proj",
    )(x, g, scale, shift, w, qg, kg, bd, cos, sin)


def _mla_proj_kernel(x_ref, g_ref, sc_ref, sh_ref, w_ref, qag_ref, kvag_ref, qbw_ref, qg_ref,
                     bd_ref, cos_ref, sin_ref,
                     kv_out, kpe_out, qb_out, zs_out, *, qscale, q_lora, kv_lora):
    d = x_ref.shape[1]
    h = _modulated_norm(x_ref[...], g_ref[...], sc_ref[...], sh_ref[...]).astype(BF16)
    n_a = q_lora + kv_lora + LANES

    a = _dot(h, w_ref[:, 0:n_a])
    qa = _row_norm(a[:, 0:q_lora], qag_ref[...]).astype(BF16)
    kv_out[...] = _row_norm(a[:, q_lora:q_lora + kv_lora], kvag_ref[...])
    kpe_out[...] = a[:, q_lora + kv_lora:q_lora + kv_lora + kpe_out.shape[1]]

    zs_out[...] = _silu(_dot(h, w_ref[:, n_a:n_a + d])).astype(BF16)

    q = _dot(qa, qbw_ref[...])
    inv_dim = 1.0 / (C_NOPE + C_ROPE)
    q = q * lax.rsqrt(_group_sumsq(q, bd_ref[...]) * inv_dim + EPS) * qg_ref[...]
    qb_out[...] = (_rope_lanes(q, cos_ref[...], sin_ref[...], C_ROPE // 2) * qscale).astype(BF16)


def _mla_proj(x, g, scale, shift, w, qag, kvag, qbw, qg, bd, cos, sin, qscale):
    b, t, d = x.shape
    tm = _row_tile(t)
    q_lora = qag.shape[1]
    kv_lora = kvag.shape[1]
    nslab = qbw.shape[1]
    table_spec = pl.BlockSpec((tm, LANES), lambda b_, i: (i, 0))
    return pl.pallas_call(
        functools.partial(_mla_proj_kernel, qscale=qscale, q_lora=q_lora, kv_lora=kv_lora),
        grid=(b, t // tm),
        in_specs=[_row_spec(tm, d), _const_spec((1, d)), _batch_spec(d), _batch_spec(d),
                  _const_spec(w.shape), _const_spec((1, q_lora)), _const_spec((1, kv_lora)),
                  _const_spec(qbw.shape), _const_spec((1, nslab)),
                  _const_spec((MXU_DIM, MXU_DIM)), table_spec, table_spec],
        out_specs=[_row_spec(tm, kv_lora), _row_spec(tm, C_ROPE), _row_spec(tm, nslab),
                   _row_spec(tm, d)],
        out_shape=[jax.ShapeDtypeStruct((b, t, kv_lora), F32),
                   jax.ShapeDtypeStruct((b, t, C_ROPE), F32),
                   jax.ShapeDtypeStruct((b, t, nslab), BF16),
                   jax.ShapeDtypeStruct((b, t, d), BF16)],
        compiler_params=_cparams("parallel", "arbitrary"),
        name="mla_in_proj",
    )(x, g, scale, shift, w, qag, kvag, qbw, qg, bd, cos, sin)


def _mla_key_kernel(kv_ref, kpe_ref, wk_ref, wv_ref, kg_ref, bd_ref, cos_ref, sin_ref,
                    kb_out, vb_out):
    kv = kv_ref[...].astype(BF16)
    vb_out[...] = _dot(kv, wv_ref[...]).astype(BF16)
    kn = _dot(kv, wk_ref[...])
    kpe = kpe_ref[...]
    k = jnp.concatenate([kn[:, c * LANES:(c + 1) * LANES] + kpe
                         for c in range(kn.shape[1] // LANES)], axis=1)
    inv_dim = 1.0 / (C_NOPE + C_ROPE)
    k = k * lax.rsqrt(_group_sumsq(k, bd_ref[...]) * inv_dim + EPS) * kg_ref[...]
    kb_out[...] = _rope_lanes(k, cos_ref[...], sin_ref[...], C_ROPE // 2).astype(BF16)


def _mla_key(kv_all, kpe_all, wk, wv, kg, bd, cos, sin):
    b, tk, kv_lora = kv_all.shape
    tm = _row_tile(tk)
    nslab = wk.shape[1]
    nv = wv.shape[1]
    table_spec = pl.BlockSpec((tm, LANES), lambda b_, i: (i, 0))
    return pl.pallas_call(
        _mla_key_kernel,
        grid=(b, tk // tm),
        in_specs=[_row_spec(tm, kv_lora), _row_spec(tm, LANES), _const_spec(wk.shape),
                  _const_spec(wv.shape), _const_spec((1, nslab)),
                  _const_spec((MXU_DIM, MXU_DIM)), table_spec, table_spec],
        out_specs=[_row_spec(tm, nslab), _row_spec(tm, nv)],
        out_shape=[jax.ShapeDtypeStruct((b, tk, nslab), BF16),
                   jax.ShapeDtypeStruct((b, tk, nv), BF16)],
        compiler_params=_cparams("parallel", "arbitrary"),
        name="mla_key_proj",
    )(kv_all, kpe_all, wk, wv, kg, bd, cos, sin)


def _out_proj_kernel(x_ref, u_ref, w_ref, gate_ref, o_ref):
    o_ref[...] = x_ref[...] + gate_ref[...] * _dot(u_ref[...], w_ref[...])


def _out_proj(x, u, w, gate):
    b, t, d = x.shape
    tm = _row_tile(t)
    return pl.pallas_call(
        _out_proj_kernel,
        grid=(b, t // tm),
        in_specs=[_row_spec(tm, d), _row_spec(tm, d), _const_spec(w.shape), _batch_spec(d)],
        out_specs=_row_spec(tm, d),
        out_shape=jax.ShapeDtypeStruct((b, t, d), F32),
        compiler_params=_cparams("parallel", "arbitrary"),
        name="out_proj_residual",
    )(x, u, w, gate)


def _split3_bf16(x):
    hi = x.astype(BF16)
    r = x - hi.astype(F32)
    mid = r.astype(BF16)
    lo = (r - mid.astype(F32)).astype(BF16)
    return hi, mid, lo


def _cumsum_kernel(x_ref, tri_ref, place_ref, cum_out, ext_out, carry_ref):
    @pl.when(pl.program_id(1) == 0)
    def _():
        carry_ref[...] = jnp.zeros_like(carry_ref)

    tri = tri_ref[...]
    c = carry_ref[...]
    for part in _split3_bf16(x_ref[...]):
        c = c + _dot(tri, part)
    cum_out[...] = c
    carry_ref[...] = c[c.shape[0] - 1:, :]
    ext = jnp.zeros(ext_out.shape, F32)
    for i, part in enumerate(_split3_bf16(c * LOG2E)):
        ext = ext + _dot(part, place_ref[i])
    ext_out[...] = ext.astype(BF16)


def _cumsum_rows(x, heads):
    b, t, n = x.shape
    tc = CUMSUM_TILE if t % CUMSUM_TILE == 0 else t
    tri = jnp.asarray(np.tril(np.ones((tc, tc), np.float32)), BF16)
    place = np.zeros((3, n, n), np.float32)
    for part in range(3):
        place[part, np.arange(heads), 3 * np.arange(heads) + part] = 1.0
    return pl.pallas_call(
        _cumsum_kernel,
        grid=(b, t // tc),
        in_specs=[_row_spec(tc, n), _const_spec((tc, tc)), _const_spec((3, n, n))],
        out_specs=[_row_spec(tc, n), _row_spec(tc, n)],
        out_shape=[jax.ShapeDtypeStruct((b, t, n), F32), jax.ShapeDtypeStruct((b, t, n), BF16)],
        scratch_shapes=[pltpu.VMEM((1, n), F32)],
        compiler_params=_cparams("parallel", "arbitrary"),
        name="logf_cumsum",
    )(x, tri, jnp.asarray(place, BF16))


def _attn_kernel(*refs, mode, tq, tk, nkb, past, tk_valid, lam_init):
    if mode == "fox":
        q_ref, k_ref, v_ref, zs_ref, kx_ref, qc_ref, o_ref = refs[:7]
    elif mode == "diff":
        q_ref, k_ref, v_ref, zs_ref, lam_ref, subg_ref, o_ref = refs[:7]
    else:
        q_ref, k_ref, v_ref, zs_ref, o_ref = refs[:5]
    vt_ref, kmax_ref, m_ref, l_ref, acc_ref = refs[-5:]
    stream_lanes = LANES if mode == "mla" else LANES // 2

    @pl.when(pl.program_id(2) == 0)
    def _():
        shift = stream_lanes.bit_length() - 1
        lane_i = lax.broadcasted_iota(jnp.int32, (LANES, LANES), 0) >> shift
        lane_j = lax.broadcasted_iota(jnp.int32, (LANES, LANES), 1) >> shift
        same_stream = jnp.where(lane_i == lane_j, 1.0, 0.0).astype(BF16)
        kmax = jnp.zeros(kmax_ref.shape, F32)
        for c in range(nkb):
            vt_ref[c] = v_ref[c * tk:(c + 1) * tk, :].astype(F32).T.astype(BF16)
            kb = k_ref[c * tk:(c + 1) * tk, :].astype(F32)
            sq = (kb * kb).astype(BF16)
            norms = jnp.concatenate(
                [_dot(sq[:, g * LANES:(g + 1) * LANES], same_stream)
                 for g in range(sq.shape[1] // LANES)], axis=1)
            kmax = jnp.maximum(kmax, jnp.max(norms, axis=0, keepdims=True))
        kmax_ref[...] = kmax

    q0 = pl.program_id(2) * tq
    row = lax.broadcasted_iota(jnp.int32, (LANES, 1), 0)
    low = row < (LANES // 2)

    qt = q_ref[...].astype(F32).T
    if mode == "mla":
        rhs = (qt[0:LANES].astype(BF16), qt[LANES:2 * LANES].astype(BF16))
    else:
        rhs = (jnp.where(low, qt, 0.0).astype(BF16), jnp.where(low, 0.0, qt).astype(BF16))

    bounds = []
    for h in range(2):
        qf = rhs[h].astype(F32)
        qn2 = jnp.sum(qf * qf, axis=0, keepdims=True)
        kn2 = kmax_ref[:, h * stream_lanes:h * stream_lanes + 1]
        bounds.append(jnp.sqrt(qn2 * kn2) * BOUND_SLACK_REL + BOUND_SLACK_ABS)
    fixed_offset = jnp.max(jnp.maximum(bounds[0], bounds[1])) <= FIXED_OFFSET_MAX_BOUND

    if mode == "fox":
        pair = pl.program_id(1)
        picks = []
        for h in range(2):
            first = 3 * (2 * pair + h)
            sel = jnp.logical_and(row >= first, row < first + 3)
            picks.append(jnp.broadcast_to(jnp.where(sel, -1.0, 0.0), (LANES, tq)).astype(BF16))
        rhs = tuple(jnp.concatenate([rhs[h], picks[h]], axis=0) for h in range(2))
        qcl = tuple(qc_ref[h:h + 1, :] * LOG2E for h in range(2))
    else:
        qcl = (None, None)

    m_ref[...] = jnp.full(m_ref.shape, NEG_INF, F32)
    l_ref[...] = jnp.zeros(l_ref.shape, F32)
    acc_ref[...] = jnp.zeros(acc_ref.shape, F32)

    q_pos = past + q0 + lax.broadcasted_iota(jnp.int32, (1, tq), 1)
    if mode == "fox":
        offsets = tuple(bounds[h] - qcl[h] for h in range(2))
    else:
        offsets = tuple(bounds)

    def step(j, masked, fixed, width=1):
        nk = width * tk
        k0 = pl.multiple_of(j * nk, nk)
        vts = [vt_ref[j * width + w] for w in range(width)]
        vt = vts[0] if width == 1 else jnp.concatenate(vts, axis=1)
        if mode == "mla":
            lhs = (k_ref[pl.ds(k0, nk), 0:LANES], k_ref[pl.ds(k0, nk), LANES:2 * LANES])
        elif mode == "fox":
            kk = jnp.concatenate([k_ref[pl.ds(k0, nk), :], kx_ref[pl.ds(k0, nk), :]], axis=1)
            lhs = (kk, kk)
        else:
            kk = k_ref[pl.ds(k0, nk), :]
            lhs = (kk, kk)
        if masked:
            k_pos = k0 + lax.broadcasted_iota(jnp.int32, (nk, 1), 0)
            if mode == "fox":
                mask = k_pos <= q_pos
            else:
                mask = (k_pos >> CHUNK_SHIFT) <= (q_pos >> CHUNK_SHIFT)
                if tk_valid < nkb * tk:
                    mask = jnp.logical_and(mask, k_pos < tk_valid)
        for h in range(2):
            s = _dot(lhs[h], rhs[h])
            if masked:
                s = jnp.where(mask, s, NEG_INF)
            if fixed:
                p = jnp.exp2(s - offsets[h])
                l_ref[h] = l_ref[h] + jnp.sum(p, axis=0, keepdims=True)
                acc_ref[h] = acc_ref[h] + _dot(vt, p.astype(BF16))
                continue
            m_old = m_ref[h]
            col_max = jnp.max(s, axis=0, keepdims=True)
            if mode == "fox":
                m_new = jnp.maximum(m_old, col_max + qcl[h])
                p = jnp.exp2(s - (m_new - qcl[h]))
            else:
                m_new = jnp.maximum(m_old, col_max)
                p = jnp.exp2(s - m_new)
            alpha = jnp.exp2(m_old - m_new)
            l_ref[h] = alpha * l_ref[h] + jnp.sum(p, axis=0, keepdims=True)
            acc_ref[h] = alpha * acc_ref[h] + _dot(vt, p.astype(BF16))
            m_ref[h] = m_new

    if mode == "fox":
        n_full = (past + q0 + 1) // tk
        j_last = (past + q0 + tq - 1) // tk
    else:
        n_full = ((past + q0) // CHUNK + 1) * CHUNK // tk
        j_last = ((past + q0 + tq - 1) // CHUNK * CHUNK + CHUNK - 1) // tk
    n_full = jnp.minimum(n_full, tk_valid // tk)
    j_last = jnp.minimum(j_last, nkb - 1)

    def sweep(fixed):
        def loop(lo, hi, masked, width=1):
            def body(j, carry):
                step(j, masked, fixed, width)
                return carry
            lax.fori_loop(lo, hi, body, 0)

        if fixed and nkb >= FIXED_WIDE:
            n_wide = n_full // FIXED_WIDE
            loop(0, n_wide, False, FIXED_WIDE)
            loop(n_wide * FIXED_WIDE, n_full, False)
        else:
            loop(0, n_full, False)
        loop(n_full, j_last + 1, True)

    @pl.when(fixed_offset)
    def _():
        sweep(True)

    @pl.when(jnp.logical_not(fixed_offset))
    def _():
        sweep(False)

    o0 = acc_ref[0] / l_ref[0]
    o1 = acc_ref[1] / l_ref[1]
    if mode == "diff":
        lv = lam_ref[...]
        lam = (jnp.exp(jnp.sum(lv[0:1] * lv[1:2], axis=-1, keepdims=True))
               - jnp.exp(jnp.sum(lv[2:3] * lv[3:4], axis=-1, keepdims=True)) + lam_init)
        ot = o0 - lam * o1
        ot = ot * lax.rsqrt(jnp.mean(ot * ot, axis=0, keepdims=True) + EPS)
        ot = ot * subg_ref[...] * (1.0 - lam_init)
    else:
        ot = jnp.where(low, o0, o1)
    o_ref[...] = (ot.T * zs_ref[...].astype(F32)).astype(BF16)


def _attention(mode, q, k_all, v_all, zs, extras, *, past, tk_valid, lam_init=0.0):
    b, t, d = zs.shape
    tk_total = k_all.shape[1]
    if past == 0:
        tq = min(ATTN_TILE, t)
        tk = tq
        t_run = t
    else:
        tq = LANES
        tk = tk_total
        t_run = -(-t // tq) * tq
        q = _pad_rows(q, t_run)
        zs = _pad_rows(zs, t_run)
    nkb = tk_total // tk
    groups = d // LANES
    qw = q.shape[2] // groups

    in_specs = [
        pl.BlockSpec((None, tq, qw), lambda b_, g, i: (b_, i, g)),
        pl.BlockSpec((None, tk_total, qw), lambda b_, g, i: (b_, 0, g)),
        pl.BlockSpec((None, tk_total, LANES), lambda b_, g, i: (b_, 0, g)),
        pl.BlockSpec((None, tq, LANES), lambda b_, g, i: (b_, i, g)),
    ]
    if mode == "fox":
        kx, qc = extras
        extras = (kx, jnp.pad(qc, ((0, 0), (0, 0), (0, 0), (0, t_run - t)), mode="edge"))
        in_specs += [
            pl.BlockSpec((None, tk_total, LANES), lambda b_, g, i: (b_, 0, 0)),
            pl.BlockSpec((None, None, 2, tq), lambda b_, g, i: (b_, g, 0, i)),
        ]
    elif mode == "diff":
        lam_vecs, sub_g = extras
        in_specs += [
            pl.BlockSpec(lam_vecs.shape, lambda b_, g, i: (0, 0)),
            pl.BlockSpec(sub_g.shape, lambda b_, g, i: (0, 0)),
        ]
    kern = functools.partial(_attn_kernel, mode=mode, tq=tq, tk=tk, nkb=nkb, past=past,
                             tk_valid=tk_valid, lam_init=lam_init)
    u = pl.pallas_call(
        kern,
        grid=(b, groups, t_run // tq),
        in_specs=in_specs,
        out_specs=pl.BlockSpec((None, tq, LANES), lambda b_, g, i: (b_, i, g)),
        out_shape=jax.ShapeDtypeStruct((b, t_run, d), BF16),
        scratch_shapes=[pltpu.VMEM((nkb, LANES, tk), BF16), pltpu.VMEM((1, qw), F32),
                        pltpu.VMEM((2, 1, tq), F32), pltpu.VMEM((2, 1, tq), F32),
                        pltpu.VMEM((2, LANES, tq), F32)],
        compiler_params=_cparams("parallel", "parallel", "arbitrary"),
        name=mode + "_attention",
    )(q, k_all, v_all, zs, *extras)
    return u[:, :t]


def _group_matrix(group):
    idx = np.arange(MXU_DIM) // group
    return jnp.asarray((idx[:, None] == idx[None, :]).astype(np.float32), BF16)


def _rope_tables(pos, half, lead, trail):
    inv = ROPE_THETA ** (-jnp.arange(half, dtype=F32) / half)
    ang = pos.astype(F32)[:, None] * inv
    cos = jnp.cos(ang)
    sin = jnp.sin(ang)
    reps = (LANES - lead - trail) // (2 * half)
    n = pos.shape[0]
    cos_t = jnp.concatenate([jnp.ones((n, lead), F32)] + [cos, cos] * reps
                            + [jnp.ones((n, trail), F32)], axis=1)
    sin_t = jnp.concatenate([jnp.zeros((n, lead), F32)] + [-sin, sin] * reps
                            + [jnp.zeros((n, trail), F32)], axis=1)
    return cos_t, sin_t


def _pad_rows(x, rows, axis=1):
    if x.shape[axis] == rows:
        return x
    pad = [(0, 0)] * x.ndim
    pad[axis] = (0, rows - x.shape[axis])
    return jnp.pad(x, pad)


def _key_rows(past, t):
    if past == 0:
        return t
    return -(-(past + t) // DEC_KEY_TILE) * DEC_KEY_TILE


def _prep_weights(p):
    d = p["norm_g"].shape[1]
    w = {}
    a_in = p["a_in_w"]
    heads_a = d // A_HEAD_DIM
    w["a_in"] = jnp.concatenate(
        [a_in[:, :, :3 * d], a_in[:, :, 3 * d + heads_a:],
         jnp.pad(a_in[:, :, 3 * d:3 * d + heads_a], ((0, 0), (0, 0), (0, LANES - heads_a)))],
        axis=2).astype(BF16)
    w["a_fb"] = jnp.pad(p["a_f_b"], ((0, 0), (0, LANES - heads_a)))[:, None, :]
    w["a_qg"] = jnp.tile(p["a_q_g"], (1, heads_a))[:, None, :]
    w["a_kg"] = jnp.tile(p["a_k_g"], (1, heads_a))[:, None, :]

    heads_b = d // (2 * B_HEAD_DIM)
    w["b_in"] = p["b_in_w"].astype(BF16)
    nb = p["b_q_g"].shape[0]
    w["b_qg"] = jnp.tile(p["b_q_g"].reshape(nb, 2 * B_HEAD_DIM), (1, heads_b))[:, None, :]
    w["b_kg"] = jnp.tile(p["b_k_g"].reshape(nb, 2 * B_HEAD_DIM), (1, heads_b))[:, None, :]
    w["b_lam"] = jnp.stack([p["b_lam_q1"], p["b_lam_k1"], p["b_lam_q2"], p["b_lam_k2"]], axis=1)
    w["b_subg"] = p["b_sub_g"][:, :, None]

    c_in = p["c_in_w"]
    q_lora = p["c_qa_g"].shape[1]
    kv_lora = p["c_kva_g"].shape[1]
    n0 = q_lora + kv_lora
    w["c_in"] = jnp.concatenate(
        [c_in[:, :, :n0],
         jnp.pad(c_in[:, :, n0:n0 + C_ROPE], ((0, 0), (0, 0), (0, LANES - C_ROPE))),
         c_in[:, :, n0 + C_ROPE:]], axis=2).astype(BF16)
    w["c_qag"] = p["c_qa_g"][:, None, :]
    w["c_kvag"] = p["c_kva_g"][:, None, :]
    nc = c_in.shape[0]
    hd = C_NOPE + C_ROPE
    qb = p["c_qb_w"].reshape(nc, q_lora, C_HEADS, hd)
    w["c_qb"] = jnp.pad(qb, ((0, 0), (0, 0), (0, 0), (0, LANES - hd))).reshape(
        nc, q_lora, C_HEADS * LANES).astype(BF16)
    kvb = p["c_kvb_w"].reshape(nc, kv_lora, C_HEADS, C_NOPE + C_VDIM)
    w["c_kb"] = jnp.pad(kvb[..., :C_NOPE], ((0, 0), (0, 0), (0, 0), (0, LANES - C_NOPE))).reshape(
        nc, kv_lora, C_HEADS * LANES).astype(BF16)
    w["c_vb"] = kvb[..., C_NOPE:].reshape(nc, kv_lora, C_HEADS * C_VDIM).astype(BF16)
    slab_gain = lambda g: jnp.tile(jnp.pad(g, ((0, 0), (0, LANES - hd))), (1, C_HEADS))[:, None, :]
    w["c_qg"] = slab_gain(p["c_q_g"])
    w["c_kg"] = slab_gain(p["c_k_g"])
    w["out"] = p["out_w"].astype(BF16)
    return w


def _run_trunk(x, mods, caches, p, w, past):
    b, t, d = x.shape
    depth = p["norm_g"].shape[0]
    heads_a = d // A_HEAD_DIM
    tk_valid = past + t
    tk_total = _key_rows(past, t)
    q_pos = past + jnp.arange(t)
    k_pos = jnp.arange(tk_total)
    bd64 = _group_matrix(A_HEAD_DIM)
    bd128 = _group_matrix(LANES)
    new_a, new_b, new_c = [], [], []

    def with_past(cache, new_bf):
        if past == 0:
            return new_bf
        old = cache.reshape(b, past, -1).astype(BF16)
        return _pad_rows(jnp.concatenate([old, new_bf], axis=1), tk_total)

    for i in range(depth):
        j = i // 3
        shift = mods[i][:, None, 0:d]
        scale = mods[i][:, None, d:2 * d]
        gate = mods[i][:, None, 2 * d:3 * d]
        g = p["norm_g"][i][None, :]
        if i % 3 == 0:
            qscale = A_HEAD_DIM ** -0.5 * LOG2E
            k, v, logf, qb, kb, vb, zs = _fox_proj(
                x, g, scale, shift, w["a_in"][j], w["a_fb"][j], w["a_qg"][j], w["a_kg"][j],
                bd64, qscale)
            new_a.append((k.reshape(b, t, heads_a, A_HEAD_DIM),
                          v.reshape(b, t, heads_a, A_HEAD_DIM), logf[:, :, :heads_a]))
            if past == 0:
                k_all, v_all, lf_all = kb, vb, logf
            else:
                k_all = with_past(caches[0][j], kb)
                v_all = with_past(caches[1][j], vb)
                lf_old = _pad_rows(caches[2][j], LANES, axis=2)
                lf_all = _pad_rows(jnp.concatenate([lf_old, logf], axis=1), tk_total)
            cum, kx = _cumsum_rows(lf_all, heads_a)
            qc = jnp.transpose(cum[:, past:past + t, :heads_a], (0, 2, 1)).reshape(
                b, heads_a // 2, 2, t)
            u = _attention("fox", qb, k_all, v_all, zs, (kx, qc), past=past, tk_valid=tk_valid)
        elif i % 3 == 1:
            qscale = B_HEAD_DIM ** -0.5 * LOG2E
            cos, sin = _rope_tables(q_pos, B_HEAD_DIM // 2, 0, 0)
            k, v, qb, kb, vb, zs = _diff_proj(
                x, g, scale, shift, w["b_in"][j], w["b_qg"][j], w["b_kg"][j], bd64, cos, sin,
                qscale)
            heads_b = d // (2 * B_HEAD_DIM)
            new_b.append((k.reshape(b, t, heads_b, 2, B_HEAD_DIM),
                          v.reshape(b, t, heads_b, 2 * B_HEAD_DIM)))
            k_all = with_past(None if past == 0 else caches[3][j], kb)
            v_all = with_past(None if past == 0 else caches[4][j], vb)
            lam_init = 0.8 - 0.6 * math.exp(-0.3 * i)
            u = _attention("diff", qb, k_all, v_all, zs, (w["b_lam"][j], w["b_subg"][j]),
                           past=past, tk_valid=tk_valid, lam_init=lam_init)
        else:
            qscale = (C_NOPE + C_ROPE) ** -0.5 * LOG2E
            cos_q, sin_q = _rope_tables(q_pos, C_ROPE // 2, C_NOPE, LANES - C_NOPE - C_ROPE)
            cos_k, sin_k = _rope_tables(k_pos, C_ROPE // 2, C_NOPE, LANES - C_NOPE - C_ROPE)
            kv_lat, kpe, qb, zs = _mla_proj(
                x, g, scale, shift, w["c_in"][j], w["c_qag"][j], w["c_kvag"][j], w["c_qb"][j],
                w["c_qg"][j], bd128, cos_q, sin_q, qscale)
            new_c.append((kv_lat, kpe))
            if past == 0:
                kv_all, kpe_all = kv_lat, kpe
            else:
                kv_all = _pad_rows(jnp.concatenate([caches[5][j], kv_lat], axis=1), tk_total)
                kpe_all = _pad_rows(jnp.concatenate([caches[6][j], kpe], axis=1), tk_total)
            kpe_slab = jnp.pad(kpe_all, ((0, 0), (0, 0), (C_NOPE, LANES - C_NOPE - C_ROPE)))
            k_all, v_all = _mla_key(kv_all, kpe_slab, w["c_kb"][j], w["c_vb"][j], w["c_kg"][j],
                                    bd128, cos_k, sin_k)
            u = _attention("mla", qb, k_all, v_all, zs, (), past=past, tk_valid=tk_valid)
        x = _out_proj(x, u, w["out"][i], gate)

    stack = lambda rows: tuple(jnp.stack(r) for r in zip(*rows))
    return x, stack(new_a), stack(new_b), stack(new_c)


def kernel(x_prompt, x_sample, cache_a_k, cache_a_v, cache_a_logf, cache_b_k, cache_b_v, cache_c_kv, cache_c_kpe, c_prompt, c_sample, norm_g, ada_w, ada_b, out_w, a_in_w, a_f_b, a_q_g, a_k_g, b_in_w, b_q_g, b_k_g, b_lam_q1, b_lam_k1, b_lam_q2, b_lam_k2, b_sub_g, c_in_w, c_qa_g, c_kva_g, c_qb_w, c_kvb_w, c_q_g, c_k_g):
    p = dict(norm_g=norm_g, out_w=out_w, a_in_w=a_in_w, a_f_b=a_f_b, a_q_g=a_q_g, a_k_g=a_k_g,
             b_in_w=b_in_w, b_q_g=b_q_g, b_k_g=b_k_g, b_lam_q1=b_lam_q1, b_lam_k1=b_lam_k1,
             b_lam_q2=b_lam_q2, b_lam_k2=b_lam_k2, b_sub_g=b_sub_g, c_in_w=c_in_w,
             c_qa_g=c_qa_g, c_kva_g=c_kva_g, c_qb_w=c_qb_w, c_kvb_w=c_kvb_w, c_q_g=c_q_g,
             c_k_g=c_k_g)
    w = _prep_weights(p)
    bp = x_prompt.shape[0]
    mods = _ada(jnp.concatenate([c_prompt, c_sample], axis=0), ada_w, ada_b)
    caches = (cache_a_k, cache_a_v, cache_a_logf, cache_b_k, cache_b_v, cache_c_kv, cache_c_kpe)
    y_p, (a_k_p, a_v_p, a_lf_p), (b_k_p, b_v_p), (c_kv_p, c_kpe_p) = _run_trunk(
        x_prompt, mods[:, :bp], None, p, w, 0)
    y_s, (a_k_s, a_v_s, a_lf_s), (b_k_s, b_v_s), (c_kv_s, c_kpe_s) = _run_trunk(
        x_sample, mods[:, bp:], caches, p, w, cache_a_k.shape[2])
    return (y_p, y_s, a_k_p, a_v_p, a_lf_p, b_k_p, b_v_p, c_kv_p, c_kpe_p,
            a_k_s, a_v_s, a_lf_s, b_k_s, b_v_s, c_kv_s, c_kpe_s)
```

```python
import functools
import math

import numpy as np
import jax
import jax.numpy as jnp
from jax import lax
from jax.experimental import pallas as pl
from jax.experimental.pallas import tpu as pltpu

F32 = jnp.float32
BF16 = jnp.bfloat16

CHUNK = 64
CHUNK_SHIFT = 6
A_HEAD_DIM = 64
B_HEAD_DIM = 64
C_HEADS = 16
C_NOPE = 64
C_ROPE = 32
C_VDIM = 64
ROPE_THETA = 10000.0
EPS = 1e-6
NEG_INF = -1e30
LOG2E = 1.4426950408889634

FIXED_OFFSET_MAX_BOUND = 40.0
BOUND_SLACK_REL = 1.02
BOUND_SLACK_ABS = 1.0
FIXED_WIDTHS = (4, 2, 1)
LANES = 128
MXU_DIM = 256
V7X_VMEM_BYTES = 64 * 1024 * 1024
VMEM_LIMIT = V7X_VMEM_BYTES * 7 // 8

ROW_TILE = 512
ATTN_TILE = 512
DEC_KEY_TILE = 128
CUMSUM_TILE = 512


def _cparams(*sem):
    return pltpu.CompilerParams(dimension_semantics=sem, vmem_limit_bytes=VMEM_LIMIT)


def _dot(a, b):
    return jnp.dot(a, b, preferred_element_type=F32)


def _split_bf16(x):
    hi = x.astype(BF16)
    lo = (x - hi.astype(F32)).astype(BF16)
    return hi, lo


def _group_sumsq(x, bd):
    outs = []
    for c in range(x.shape[1] // MXU_DIM):
        xc = x[:, c * MXU_DIM:(c + 1) * MXU_DIM]
        outs.append(_dot((xc * xc).astype(BF16), bd))
    return outs[0] if len(outs) == 1 else jnp.concatenate(outs, axis=1)


def _modulated_norm(x, g, scale, shift):
    y = x * lax.rsqrt(jnp.mean(x * x, axis=-1, keepdims=True) + EPS) * g
    return y * (1.0 + scale) + shift


def _row_norm(x, g):
    return x * lax.rsqrt(jnp.mean(x * x, axis=-1, keepdims=True) + EPS) * g


def _swap_halves(x, half):
    lane = lax.broadcasted_iota(jnp.int32, (1, LANES), 1)
    first = (lane & (2 * half - 1)) < half
    return jnp.where(first, pltpu.roll(x, LANES - half, 1), pltpu.roll(x, half, 1))


def _rope_lanes(x, cos, sin, half):
    outs = []
    for c in range(x.shape[1] // LANES):
        xc = x[:, c * LANES:(c + 1) * LANES]
        outs.append(xc * cos + _swap_halves(xc, half) * sin)
    return outs[0] if len(outs) == 1 else jnp.concatenate(outs, axis=1)


def _silu(z):
    return z * jax.nn.sigmoid(z)


def _log_sigmoid(x):
    return jnp.minimum(x, 0.0) - jnp.log1p(jnp.exp(-jnp.abs(x)))


def _ada_kernel(c_ref, w_ref, b_ref, o_ref):
    c = c_ref[...]
    c_hi, c_lo = _split_bf16(_silu(c))
    w_hi, w_lo = _split_bf16(w_ref[...])
    o_ref[...] = _dot(c_hi, w_hi) + _dot(c_hi, w_lo) + _dot(c_lo, w_hi) + b_ref[...]


def _ada(c_all, ada_w, ada_b):
    depth, d, n3 = ada_w.shape
    rows = c_all.shape[0]
    tn = d
    return pl.pallas_call(
        _ada_kernel,
        grid=(depth, n3 // tn),
        in_specs=[
            pl.BlockSpec((rows, d), lambda l, j: (0, 0)),
            pl.BlockSpec((None, d, tn), lambda l, j: (l, 0, j)),
            pl.BlockSpec((None, 1, tn), lambda l, j: (l, 0, j)),
        ],
        out_specs=pl.BlockSpec((None, rows, tn), lambda l, j: (l, 0, j)),
        out_shape=jax.ShapeDtypeStruct((depth, rows, n3), F32),
        compiler_params=_cparams("arbitrary", "arbitrary"),
        name="ada_modulation",
    )(c_all, ada_w, ada_b.reshape(depth, 1, n3))


def _row_tile(t):
    sublane_pack = 16
    for tm in range(min(ROW_TILE, t), 0, -1):
        if t % tm == 0 and (tm % sublane_pack == 0 or tm == t):
            return tm
    raise ValueError(f"no row tile for {t} rows")


def _row_spec(tm, n):
    return pl.BlockSpec((None, tm, n), lambda b, i: (b, i, 0))


def _batch_spec(n):
    return pl.BlockSpec((None, 1, n), lambda b, i: (b, 0, 0))


def _const_spec(shape):
    return pl.BlockSpec(shape, lambda b, i: (0,) * len(shape))


def _fox_proj_kernel(x_ref, g_ref, sc_ref, sh_ref, w_ref, fb_ref, qg_ref, kg_ref, bd_ref,
                     k_out, v_out, lf_out, qb_out, kb_out, vb_out, zs_out, *, qscale):
    d = x_ref.shape[1]
    h = _modulated_norm(x_ref[...], g_ref[...], sc_ref[...], sh_ref[...]).astype(BF16)
    bd = bd_ref[...]
    inv_dim = 1.0 / A_HEAD_DIM

    q = _dot(h, w_ref[:, 0:d])
    q = q * lax.rsqrt(_group_sumsq(q, bd) * inv_dim + EPS) * qg_ref[...]
    qb_out[...] = (q * qscale).astype(BF16)

    k = _dot(h, w_ref[:, d:2 * d])
    k = k * lax.rsqrt(_group_sumsq(k, bd) * inv_dim + EPS) * kg_ref[...]
    k_out[...] = k
    kb_out[...] = k.astype(BF16)

    v = _dot(h, w_ref[:, 2 * d:3 * d])
    v_out[...] = v
    vb_out[...] = v.astype(BF16)

    zs_out[...] = _silu(_dot(h, w_ref[:, 3 * d:4 * d])).astype(BF16)

    f = _dot(h, w_ref[:, 4 * d:4 * d + LANES]) + fb_ref[...]
    lf_out[...] = _log_sigmoid(f)


def _fox_proj(x, g, scale, shift, w, fb, qg, kg, bd, qscale):
    b, t, d = x.shape
    tm = _row_tile(t)
    f32_out = jax.ShapeDtypeStruct((b, t, d), F32)
    bf_out = jax.ShapeDtypeStruct((b, t, d), BF16)
    return pl.pallas_call(
        functools.partial(_fox_proj_kernel, qscale=qscale),
        grid=(b, t // tm),
        in_specs=[_row_spec(tm, d), _const_spec((1, d)), _batch_spec(d), _batch_spec(d),
                  _const_spec(w.shape), _const_spec((1, LANES)), _const_spec((1, d)),
                  _const_spec((1, d)), _const_spec((MXU_DIM, MXU_DIM))],
        out_specs=[_row_spec(tm, d), _row_spec(tm, d), _row_spec(tm, LANES),
                   _row_spec(tm, d), _row_spec(tm, d), _row_spec(tm, d), _row_spec(tm, d)],
        out_shape=[f32_out, f32_out, jax.ShapeDtypeStruct((b, t, LANES), F32),
                   bf_out, bf_out, bf_out, bf_out],
        compiler_params=_cparams("parallel", "arbitrary"),
        name="fox_in_proj",
    )(x, g, scale, shift, w, fb, qg, kg, bd)


def _diff_proj_kernel(x_ref, g_ref, sc_ref, sh_ref, w_ref, qg_ref, kg_ref, bd_ref, cos_ref, sin_ref,
                      k_out, v_out, qb_out, kb_out, vb_out, zs_out, *, qscale):
    d = x_ref.shape[1]
    h = _modulated_norm(x_ref[...], g_ref[...], sc_ref[...], sh_ref[...]).astype(BF16)
    bd = bd_ref[...]
    cos = cos_ref[...]
    sin = sin_ref[...]
    inv_dim = 1.0 / B_HEAD_DIM
    half = B_HEAD_DIM // 2

    q = _dot(h, w_ref[:, 0:d])
    q = q * lax.rsqrt(_group_sumsq(q, bd) * inv_dim + EPS) * qg_ref[...]
    qb_out[...] = (_rope_lanes(q, cos, sin, half) * qscale).astype(BF16)

    k = _dot(h, w_ref[:, d:2 * d])
    k = k * lax.rsqrt(_group_sumsq(k, bd) * inv_dim + EPS) * kg_ref[...]
    k = _rope_lanes(k, cos, sin, half)
    k_out[...] = k
    kb_out[...] = k.astype(BF16)

    v = _dot(h, w_ref[:, 2 * d:3 * d])
    v_out[...] = v
    vb_out[...] = v.astype(BF16)

    zs_out[...] = _silu(_dot(h, w_ref[:, 3 * d:4 * d])).astype(BF16)


def _diff_proj(x, g, scale, shift, w, qg, kg, bd, cos, sin, qscale):
    b, t, d = x.shape
    tm = _row_tile(t)
    f32_out = jax.ShapeDtypeStruct((b, t, d), F32)
    bf_out = jax.ShapeDtypeStruct((b, t, d), BF16)
    table_spec = pl.BlockSpec((tm, LANES), lambda b_, i: (i, 0))
    return pl.pallas_call(
        functools.partial(_diff_proj_kernel, qscale=qscale),
        grid=(b, t // tm),
        in_specs=[_row_spec(tm, d), _const_spec((1, d)), _batch_spec(d), _batch_spec(d),
                  _const_spec(w.shape), _const_spec((1, d)), _const_spec((1, d)),
                  _const_spec((MXU_DIM, MXU_DIM)), table_spec, table_spec],
        out_specs=[_row_spec(tm, d)] * 6,
        out_shape=[f32_out, f32_out, bf_out, bf_out, bf_out, bf_out],
        compiler_params=_cparams("parallel", "arbitrary"),
        name="diff_in_proj",
    )(x, g, scale, shift, w, qg, kg, bd, cos, sin)


def _mla_proj_kernel(x_ref, g_ref, sc_ref, sh_ref, w_ref, qag_ref, kvag_ref, qbw_ref, qg_ref,
                     bd_ref, cos_ref, sin_ref,
                     kv_out, kpe_out, qb_out, zs_out, *, qscale, q_lora, kv_lora):
    d = x_ref.shape[1]
    h = _modulated_norm(x_ref[...], g_ref[...], sc_ref[...], sh_ref[...]).astype(BF16)
    n_a = q_lora + kv_lora + LANES

    a = _dot(h, w_ref[:, 0:n_a])
    qa = _row_norm(a[:, 0:q_lora], qag_ref[...]).astype(BF16)
    kv_out[...] = _row_norm(a[:, q_lora:q_lora + kv_lora], kvag_ref[...])
    kpe_out[...] = a[:, q_lora + kv_lora:q_lora + kv_lora + kpe_out.shape[1]]

    zs_out[...] = _silu(_dot(h, w_ref[:, n_a:n_a + d])).astype(BF16)

    q = _dot(qa, qbw_ref[...])
    inv_dim = 1.0 / (C_NOPE + C_ROPE)
    q = q * lax.rsqrt(_group_sumsq(q, bd_ref[...]) * inv_dim + EPS) * qg_ref[...]
    qb_out[...] = (_rope_lanes(q, cos_ref[...], sin_ref[...], C_ROPE // 2) * qscale).astype(BF16)


def _mla_proj(x, g, scale, shift, w, qag, kvag, qbw, qg, bd, cos, sin, qscale):
    b, t, d = x.shape
    tm = _row_tile(t)
    q_lora = qag.shape[1]
    kv_lora = kvag.shape[1]
    nslab = qbw.shape[1]
    table_spec = pl.BlockSpec((tm, LANES), lambda b_, i: (i, 0))
    return pl.pallas_call(
        functools.partial(_mla_proj_kernel, qscale=qscale, q_lora=q_lora, kv_lora=kv_lora),
        grid=(b, t // tm),
        in_specs=[_row_spec(tm, d), _const_spec((1, d)), _batch_spec(d), _batch_spec(d),
                  _const_spec(w.shape), _const_spec((1, q_lora)), _const_spec((1, kv_lora)),
                  _const_spec(qbw.shape), _const_spec((1, nslab)),
                  _const_spec((MXU_DIM, MXU_DIM)), table_spec, table_spec],
        out_specs=[_row_spec(tm, kv_lora), _row_spec(tm, C_ROPE), _row_spec(tm, nslab),
                   _row_spec(tm, d)],
        out_shape=[jax.ShapeDtypeStruct((b, t, kv_lora), F32),
                   jax.ShapeDtypeStruct((b, t, C_ROPE), F32),
                   jax.ShapeDtypeStruct((b, t, nslab), BF16),
                   jax.ShapeDtypeStruct((b, t, d), BF16)],
        compiler_params=_cparams("parallel", "arbitrary"),
        name="mla_in_proj",
    )(x, g, scale, shift, w, qag, kvag, qbw, qg, bd, cos, sin)


def _mla_key_kernel(kv_ref, kpe_ref, wk_ref, wv_ref, kg_ref, bd_ref, cos_ref, sin_ref,
                    kb_out, vb_out):
    kv = kv_ref[...].astype(BF16)
    vb_out[...] = _dot(kv, wv_ref[...]).astype(BF16)
    kn = _dot(kv, wk_ref[...])
    kpe = kpe_ref[...]
    k = jnp.concatenate([kn[:, c * LANES:(c + 1) * LANES] + kpe
                         for c in range(kn.shape[1] // LANES)], axis=1)
    inv_dim = 1.0 / (C_NOPE + C_ROPE)
    k = k * lax.rsqrt(_group_sumsq(k, bd_ref[...]) * inv_dim + EPS) * kg_ref[...]
    kb_out[...] = _rope_lanes(k, cos_ref[...], sin_ref[...], C_ROPE // 2).astype(BF16)


def _mla_key(kv_all, kpe_all, wk, wv, kg, bd, cos, sin):
    b, tk, kv_lora = kv_all.shape
    tm = _row_tile(tk)
    nslab = wk.shape[1]
    nv = wv.shape[1]
    table_spec = pl.BlockSpec((tm, LANES), lambda b_, i: (i, 0))
    return pl.pallas_call(
        _mla_key_kernel,
        grid=(b, tk // tm),
        in_specs=[_row_spec(tm, kv_lora), _row_spec(tm, LANES), _const_spec(wk.shape),
                  _const_spec(wv.shape), _const_spec((1, nslab)),
                  _const_spec((MXU_DIM, MXU_DIM)), table_spec, table_spec],
        out_specs=[_row_spec(tm, nslab), _row_spec(tm, nv)],
        out_shape=[jax.ShapeDtypeStruct((b, tk, nslab), BF16),
                   jax.ShapeDtypeStruct((b, tk, nv), BF16)],
        compiler_params=_cparams("parallel", "arbitrary"),
        name="mla_key_proj",
    )(kv_all, kpe_all, wk, wv, kg, bd, cos, sin)


def _out_proj_kernel(x_ref, u_ref, w_ref, gate_ref, o_ref):
    o_ref[...] = x_ref[...] + gate_ref[...] * _dot(u_ref[...], w_ref[...])


def _out_proj(x, u, w, gate):
    b, t, d = x.shape
    tm = _row_tile(t)
    return pl.pallas_call(
        _out_proj_kernel,
        grid=(b, t // tm),
        in_specs=[_row_spec(tm, d), _row_spec(tm, d), _const_spec(w.shape), _batch_spec(d)],
        out_specs=_row_spec(tm, d),
        out_shape=jax.ShapeDtypeStruct((b, t, d), F32),
        compiler_params=_cparams("parallel", "arbitrary"),
        name="out_proj_residual",
    )(x, u, w, gate)


def _split3_bf16(x):
    hi = x.astype(BF16)
    r = x - hi.astype(F32)
    mid = r.astype(BF16)
    lo = (r - mid.astype(F32)).astype(BF16)
    return hi, mid, lo


def _cumsum_kernel(x_ref, tri_ref, place_ref, cum_out, ext_out, carry_ref):
    @pl.when(pl.program_id(1) == 0)
    def _():
        carry_ref[...] = jnp.zeros_like(carry_ref)

    tri = tri_ref[...]
    c = carry_ref[...]
    for part in _split3_bf16(x_ref[...]):
        c = c + _dot(tri, part)
    cum_out[...] = c
    carry_ref[...] = c[c.shape[0] - 1:, :]
    ext = jnp.zeros(ext_out.shape, F32)
    for i, part in enumerate(_split3_bf16(c * LOG2E)):
        ext = ext + _dot(part, place_ref[i])
    ext_out[...] = ext.astype(BF16)


def _cumsum_rows(x, heads):
    b, t, n = x.shape
    tc = CUMSUM_TILE if t % CUMSUM_TILE == 0 else t
    tri = jnp.asarray(np.tril(np.ones((tc, tc), np.float32)), BF16)
    place = np.zeros((3, n, n), np.float32)
    for part in range(3):
        place[part, np.arange(heads), 3 * np.arange(heads) + part] = 1.0
    return pl.pallas_call(
        _cumsum_kernel,
        grid=(b, t // tc),
        in_specs=[_row_spec(tc, n), _const_spec((tc, tc)), _const_spec((3, n, n))],
        out_specs=[_row_spec(tc, n), _row_spec(tc, n)],
        out_shape=[jax.ShapeDtypeStruct((b, t, n), F32), jax.ShapeDtypeStruct((b, t, n), BF16)],
        scratch_shapes=[pltpu.VMEM((1, n), F32)],
        compiler_params=_cparams("parallel", "arbitrary"),
        name="logf_cumsum",
    )(x, tri, jnp.asarray(place, BF16))


def _attn_kernel(*refs, mode, tq, tk, nkb, past, tk_valid, lam_init):
    if mode == "fox":
        qn_ref, q_ref, k_ref, v_ref, zs_ref, kx_ref, qc_ref, o_ref = refs[:8]
    elif mode == "diff":
        qn_ref, q_ref, k_ref, v_ref, zs_ref, lam_ref, subg_ref, o_ref = refs[:8]
    else:
        qn_ref, q_ref, k_ref, v_ref, zs_ref, o_ref = refs[:6]
    vt_ref, bound_ref, m_ref, l_ref, acc_ref = refs[-5:]
    stream_lanes = LANES if mode == "mla" else LANES // 2
    vrows = acc_ref.shape[1]

    @pl.when(pl.program_id(2) == 0)
    def _():
        shift = stream_lanes.bit_length() - 1
        lane_i = lax.broadcasted_iota(jnp.int32, (LANES, LANES), 0) >> shift
        lane_j = lax.broadcasted_iota(jnp.int32, (LANES, LANES), 1) >> shift
        same_stream = jnp.where(lane_i == lane_j, 1.0, 0.0).astype(BF16)
        kmax = jnp.zeros((1, k_ref.shape[1]), F32)
        for c in range(nkb):
            vt_ref[c] = v_ref[c * tk:(c + 1) * tk, :].astype(F32).T.astype(BF16)
            kb = k_ref[c * tk:(c + 1) * tk, :].astype(F32)
            sq = (kb * kb).astype(BF16)
            norms = jnp.concatenate(
                [_dot(sq[:, g * LANES:(g + 1) * LANES], same_stream)
                 for g in range(sq.shape[1] // LANES)], axis=1)
            kmax = jnp.maximum(kmax, jnp.max(norms, axis=0, keepdims=True))
        bound = jnp.sqrt(kmax) * (qn_ref[0, 0] * BOUND_SLACK_REL) + BOUND_SLACK_ABS
        for h in range(2):
            bound_ref[h] = jnp.max(bound[:, h * stream_lanes:h * stream_lanes + 1])

    q0 = pl.program_id(2) * tq
    row = lax.broadcasted_iota(jnp.int32, (LANES, 1), 0)
    low = row < (LANES // 2)

    qt = q_ref[...].astype(F32).T
    if mode == "mla":
        rhs = (qt[0:LANES].astype(BF16), qt[LANES:2 * LANES].astype(BF16))
    else:
        rhs = (jnp.where(low, qt, 0.0).astype(BF16), jnp.where(low, 0.0, qt).astype(BF16))

    bounds = [jnp.full((1, tq), bound_ref[h], F32) for h in range(2)]
    fixed_offset = jnp.maximum(bound_ref[0], bound_ref[1]) <= FIXED_OFFSET_MAX_BOUND

    if mode == "fox":
        pair = pl.program_id(1)
        picks = []
        for h in range(2):
            first = 3 * (2 * pair + h)
            sel = jnp.logical_and(row >= first, row < first + 3)
            picks.append(jnp.broadcast_to(jnp.where(sel, -1.0, 0.0), (LANES, tq)).astype(BF16))
        rhs = tuple(jnp.concatenate([rhs[h], picks[h]], axis=0) for h in range(2))
        qcl = tuple(qc_ref[h:h + 1, :] * LOG2E for h in range(2))
    else:
        qcl = (None, None)

    m_ref[...] = jnp.full(m_ref.shape, NEG_INF, F32)
    l_ref[...] = jnp.zeros(l_ref.shape, F32)
    acc_ref[...] = jnp.zeros(acc_ref.shape, F32)

    if mode == "mla":
        zero = jnp.zeros_like(rhs[0])
        rhs_all = jnp.concatenate([jnp.concatenate([rhs[0], zero], axis=0),
                                   jnp.concatenate([zero, rhs[1]], axis=0)], axis=1)
    else:
        rhs_all = jnp.concatenate(rhs, axis=1)
    q_pos = past + q0 + lax.broadcasted_iota(jnp.int32, (1, tq), 1)
    q_pos = jnp.concatenate([q_pos, q_pos], axis=1)
    bound_all = jnp.concatenate(bounds, axis=1)
    if mode == "fox":
        qcl = jnp.concatenate(qcl, axis=1)
        offset_all = bound_all - qcl
    else:
        offset_all = bound_all
    stream_cols = (slice(0, tq), slice(tq, 2 * tq))

    def step(j, masked, fixed, width=1):
        nk = width * tk
        k0 = pl.multiple_of(j * nk, nk)

        def values(h, w, lo, n):
            rows = slice(0, LANES) if mode == "diff" else slice(h * vrows, (h + 1) * vrows)
            return vt_ref[j * width + w, rows, lo:lo + n]

        def scores(lo, n):
            ks = pl.ds(pl.multiple_of(k0 + lo, n), n)
            if mode == "fox":
                lhs = jnp.concatenate([k_ref[ks, :], kx_ref[ks, :]], axis=1)
            else:
                lhs = k_ref[ks, :]
            s = _dot(lhs, rhs_all)
            if masked:
                k_pos = k0 + lo + lax.broadcasted_iota(jnp.int32, (n, 1), 0)
                if mode == "fox":
                    mask = k_pos <= q_pos
                else:
                    mask = (k_pos >> CHUNK_SHIFT) <= (q_pos >> CHUNK_SHIFT)
                    if tk_valid < nkb * tk:
                        mask = jnp.logical_and(mask, k_pos < tk_valid)
                s = jnp.where(mask, s, NEG_INF)
            return s

        s = scores(0, nk)
        vt = [jnp.concatenate([values(h, w, 0, tk) for w in range(width)], axis=1)
              if width > 1 else values(h, 0, 0, tk) for h in range(2)]
        if fixed:
            p = jnp.exp2(s - offset_all)
            l_ref[...] = l_ref[...] + jnp.sum(p, axis=0, keepdims=True)
            p = p.astype(BF16)
            for h in range(2):
                acc_ref[h] = acc_ref[h] + _dot(vt[h], p[:, stream_cols[h]])
            return
        m_old = m_ref[...]
        col_max = jnp.max(s, axis=0, keepdims=True)
        if mode == "fox":
            m_new = jnp.maximum(m_old, col_max + qcl)
            p = jnp.exp2(s - (m_new - qcl))
        else:
            m_new = jnp.maximum(m_old, col_max)
            p = jnp.exp2(s - m_new)
        alpha = jnp.exp2(m_old - m_new)
        l_ref[...] = alpha * l_ref[...] + jnp.sum(p, axis=0, keepdims=True)
        m_ref[...] = m_new
        p = p.astype(BF16)
        for h in range(2):
            cols = stream_cols[h]
            acc_ref[h] = alpha[:, cols] * acc_ref[h] + _dot(vt[h], p[:, cols])

    if mode == "fox":
        n_full = (past + q0 + 1) // tk
        j_last = (past + q0 + tq - 1) // tk
    else:
        n_full = ((past + q0) // CHUNK + 1) * CHUNK // tk
        j_last = ((past + q0 + tq - 1) // CHUNK * CHUNK + CHUNK - 1) // tk
    n_full = jnp.minimum(n_full, tk_valid // tk)
    j_last = jnp.minimum(j_last, nkb - 1)

    def sweep(fixed):
        def loop(lo, hi, masked, width=1):
            def body(j, carry):
                step(j, masked, fixed, width)
                return carry
            lax.fori_loop(lo, hi, body, 0)

        if fixed:
            start = 0
            for width in FIXED_WIDTHS:
                if width <= nkb:
                    trips = (n_full - start) // width
                    loop(start // width, start // width + trips, False, width)
                    start = start + trips * width
        else:
            loop(0, n_full, False)
        loop(n_full, j_last + 1, True)

    @pl.when(fixed_offset)
    def _():
        sweep(True)

    @pl.when(jnp.logical_not(fixed_offset))
    def _():
        sweep(False)

    o0 = acc_ref[0] / l_ref[:, stream_cols[0]]
    o1 = acc_ref[1] / l_ref[:, stream_cols[1]]
    if mode == "diff":
        lv = lam_ref[...]
        lam = (jnp.exp(jnp.sum(lv[0:1] * lv[1:2], axis=-1, keepdims=True))
               - jnp.exp(jnp.sum(lv[2:3] * lv[3:4], axis=-1, keepdims=True)) + lam_init)
        ot = o0 - lam * o1
        ot = ot * lax.rsqrt(jnp.mean(ot * ot, axis=0, keepdims=True) + EPS)
        ot = ot * subg_ref[...] * (1.0 - lam_init)
    else:
        ot = jnp.concatenate([o0, o1], axis=0)
    o_ref[...] = (ot.T * zs_ref[...].astype(F32)).astype(BF16)


def _q_norm_bound(gain, width, qscale):
    return math.sqrt(width) * jnp.max(jnp.abs(gain)) * qscale


def _attention(mode, q, k_all, v_all, zs, extras, *, past, tk_valid, q_norm_bound, lam_init=0.0):
    b, t, d = zs.shape
    tk_total = k_all.shape[1]
    if past == 0:
        tq = min(ATTN_TILE, t)
        tk = tq
        t_run = t
    else:
        tq = LANES
        tk = tk_total
        t_run = -(-t // tq) * tq
        q = _pad_rows(q, t_run)
        zs = _pad_rows(zs, t_run)
    nkb = tk_total // tk
    groups = d // LANES
    qw = q.shape[2] // groups

    in_specs = [
        pl.BlockSpec(memory_space=pltpu.SMEM),
        pl.BlockSpec((None, tq, qw), lambda b_, g, i: (b_, i, g)),
        pl.BlockSpec((None, tk_total, qw), lambda b_, g, i: (b_, 0, g)),
        pl.BlockSpec((None, tk_total, LANES), lambda b_, g, i: (b_, 0, g)),
        pl.BlockSpec((None, tq, LANES), lambda b_, g, i: (b_, i, g)),
    ]
    if mode == "fox":
        kx, qc = extras
        extras = (kx, jnp.pad(qc, ((0, 0), (0, 0), (0, 0), (0, t_run - t)), mode="edge"))
        in_specs += [
            pl.BlockSpec((None, tk_total, LANES), lambda b_, g, i: (b_, 0, 0)),
            pl.BlockSpec((None, None, 2, tq), lambda b_, g, i: (b_, g, 0, i)),
        ]
    elif mode == "diff":
        lam_vecs, sub_g = extras
        in_specs += [
            pl.BlockSpec(lam_vecs.shape, lambda b_, g, i: (0, 0)),
            pl.BlockSpec(sub_g.shape, lambda b_, g, i: (0, 0)),
        ]
    kern = functools.partial(_attn_kernel, mode=mode, tq=tq, tk=tk, nkb=nkb, past=past,
                             tk_valid=tk_valid, lam_init=lam_init)
    u = pl.pallas_call(
        kern,
        grid=(b, groups, t_run // tq),
        in_specs=in_specs,
        out_specs=pl.BlockSpec((None, tq, LANES), lambda b_, g, i: (b_, i, g)),
        out_shape=jax.ShapeDtypeStruct((b, t_run, d), BF16),
        scratch_shapes=[pltpu.VMEM((nkb, LANES, tk), BF16), pltpu.SMEM((2,), F32),
                        pltpu.VMEM((1, 2 * tq), F32), pltpu.VMEM((1, 2 * tq), F32),
                        pltpu.VMEM((2, LANES if mode == "diff" else LANES // 2, tq), F32)],
        compiler_params=_cparams("parallel", "parallel", "arbitrary"),
        name=mode + "_attention",
    )(jnp.reshape(q_norm_bound, (1, 1)).astype(F32), q, k_all, v_all, zs, *extras)
    return u[:, :t]


def _group_matrix(group):
    idx = np.arange(MXU_DIM) // group
    return jnp.asarray((idx[:, None] == idx[None, :]).astype(np.float32), BF16)


def _rope_tables(pos, half, lead, trail):
    inv = ROPE_THETA ** (-jnp.arange(half, dtype=F32) / half)
    ang = pos.astype(F32)[:, None] * inv
    cos = jnp.cos(ang)
    sin = jnp.sin(ang)
    reps = (LANES - lead - trail) // (2 * half)
    n = pos.shape[0]
    cos_t = jnp.concatenate([jnp.ones((n, lead), F32)] + [cos, cos] * reps
                            + [jnp.ones((n, trail), F32)], axis=1)
    sin_t = jnp.concatenate([jnp.zeros((n, lead), F32)] + [-sin, sin] * reps
                            + [jnp.zeros((n, trail), F32)], axis=1)
    return cos_t, sin_t


def _pad_rows(x, rows, axis=1):
    if x.shape[axis] == rows:
        return x
    pad = [(0, 0)] * x.ndim
    pad[axis] = (0, rows - x.shape[axis])
    return jnp.pad(x, pad)


def _key_rows(past, t):
    if past == 0:
        return t
    return -(-(past + t) // DEC_KEY_TILE) * DEC_KEY_TILE


def _prep_weights(p):
    d = p["norm_g"].shape[1]
    w = {}
    a_in = p["a_in_w"]
    heads_a = d // A_HEAD_DIM
    w["a_in"] = jnp.concatenate(
        [a_in[:, :, :3 * d], a_in[:, :, 3 * d + heads_a:],
         jnp.pad(a_in[:, :, 3 * d:3 * d + heads_a], ((0, 0), (0, 0), (0, LANES - heads_a)))],
        axis=2).astype(BF16)
    w["a_fb"] = jnp.pad(p["a_f_b"], ((0, 0), (0, LANES - heads_a)))[:, None, :]
    w["a_qg"] = jnp.tile(p["a_q_g"], (1, heads_a))[:, None, :]
    w["a_kg"] = jnp.tile(p["a_k_g"], (1, heads_a))[:, None, :]

    heads_b = d // (2 * B_HEAD_DIM)
    w["b_in"] = p["b_in_w"].astype(BF16)
    nb = p["b_q_g"].shape[0]
    w["b_qg"] = jnp.tile(p["b_q_g"].reshape(nb, 2 * B_HEAD_DIM), (1, heads_b))[:, None, :]
    w["b_kg"] = jnp.tile(p["b_k_g"].reshape(nb, 2 * B_HEAD_DIM), (1, heads_b))[:, None, :]
    w["b_lam"] = jnp.stack([p["b_lam_q1"], p["b_lam_k1"], p["b_lam_q2"], p["b_lam_k2"]], axis=1)
    w["b_subg"] = p["b_sub_g"][:, :, None]

    c_in = p["c_in_w"]
    q_lora = p["c_qa_g"].shape[1]
    kv_lora = p["c_kva_g"].shape[1]
    n0 = q_lora + kv_lora
    w["c_in"] = jnp.concatenate(
        [c_in[:, :, :n0],
         jnp.pad(c_in[:, :, n0:n0 + C_ROPE], ((0, 0), (0, 0), (0, LANES - C_ROPE))),
         c_in[:, :, n0 + C_ROPE:]], axis=2).astype(BF16)
    w["c_qag"] = p["c_qa_g"][:, None, :]
    w["c_kvag"] = p["c_kva_g"][:, None, :]
    nc = c_in.shape[0]
    hd = C_NOPE + C_ROPE
    qb = p["c_qb_w"].reshape(nc, q_lora, C_HEADS, hd)
    w["c_qb"] = jnp.pad(qb, ((0, 0), (0, 0), (0, 0), (0, LANES - hd))).reshape(
        nc, q_lora, C_HEADS * LANES).astype(BF16)
    kvb = p["c_kvb_w"].reshape(nc, kv_lora, C_HEADS, C_NOPE + C_VDIM)
    w["c_kb"] = jnp.pad(kvb[..., :C_NOPE], ((0, 0), (0, 0), (0, 0), (0, LANES - C_NOPE))).reshape(
        nc, kv_lora, C_HEADS * LANES).astype(BF16)
    w["c_vb"] = kvb[..., C_NOPE:].reshape(nc, kv_lora, C_HEADS * C_VDIM).astype(BF16)
    slab_gain = lambda g: jnp.tile(jnp.pad(g, ((0, 0), (0, LANES - hd))), (1, C_HEADS))[:, None, :]
    w["c_qg"] = slab_gain(p["c_q_g"])
    w["c_kg"] = slab_gain(p["c_k_g"])
    w["out"] = p["out_w"].astype(BF16)
    return w


def _run_trunk(x, mods, caches, p, w, past):
    b, t, d = x.shape
    depth = p["norm_g"].shape[0]
    heads_a = d // A_HEAD_DIM
    tk_valid = past + t
    tk_total = _key_rows(past, t)
    q_pos = past + jnp.arange(t)
    k_pos = jnp.arange(tk_total)
    bd64 = _group_matrix(A_HEAD_DIM)
    bd128 = _group_matrix(LANES)
    new_a, new_b, new_c = [], [], []

    def with_past(cache, new_bf):
        if past == 0:
            return new_bf
        old = cache.reshape(b, past, -1).astype(BF16)
        return _pad_rows(jnp.concatenate([old, new_bf], axis=1), tk_total)

    for i in range(depth):
        j = i // 3
        shift = mods[i][:, None, 0:d]
        scale = mods[i][:, None, d:2 * d]
        gate = mods[i][:, None, 2 * d:3 * d]
        g = p["norm_g"][i][None, :]
        if i % 3 == 0:
            qscale = A_HEAD_DIM ** -0.5 * LOG2E
            k, v, logf, qb, kb, vb, zs = _fox_proj(
                x, g, scale, shift, w["a_in"][j], w["a_fb"][j], w["a_qg"][j], w["a_kg"][j],
                bd64, qscale)
            new_a.append((k.reshape(b, t, heads_a, A_HEAD_DIM),
                          v.reshape(b, t, heads_a, A_HEAD_DIM), logf[:, :, :heads_a]))
            if past == 0:
                k_all, v_all, lf_all = kb, vb, logf
            else:
                k_all = with_past(caches[0][j], kb)
                v_all = with_past(caches[1][j], vb)
                lf_old = _pad_rows(caches[2][j], LANES, axis=2)
                lf_all = _pad_rows(jnp.concatenate([lf_old, logf], axis=1), tk_total)
            cum, kx = _cumsum_rows(lf_all, heads_a)
            qc = jnp.transpose(cum[:, past:past + t, :heads_a], (0, 2, 1)).reshape(
                b, heads_a // 2, 2, t)
            u = _attention("fox", qb, k_all, v_all, zs, (kx, qc), past=past, tk_valid=tk_valid,
                           q_norm_bound=_q_norm_bound(p["a_q_g"][j], A_HEAD_DIM, qscale))
        elif i % 3 == 1:
            qscale = B_HEAD_DIM ** -0.5 * LOG2E
            cos, sin = _rope_tables(q_pos, B_HEAD_DIM // 2, 0, 0)
            k, v, qb, kb, vb, zs = _diff_proj(
                x, g, scale, shift, w["b_in"][j], w["b_qg"][j], w["b_kg"][j], bd64, cos, sin,
                qscale)
            heads_b = d // (2 * B_HEAD_DIM)
            new_b.append((k.reshape(b, t, heads_b, 2, B_HEAD_DIM),
                          v.reshape(b, t, heads_b, 2 * B_HEAD_DIM)))
            k_all = with_past(None if past == 0 else caches[3][j], kb)
            v_all = with_past(None if past == 0 else caches[4][j], vb)
            lam_init = 0.8 - 0.6 * math.exp(-0.3 * i)
            u = _attention("diff", qb, k_all, v_all, zs, (w["b_lam"][j], w["b_subg"][j]),
                           past=past, tk_valid=tk_valid, lam_init=lam_init,
                           q_norm_bound=_q_norm_bound(p["b_q_g"][j], B_HEAD_DIM, qscale))
        else:
            qscale = (C_NOPE + C_ROPE) ** -0.5 * LOG2E
            cos_q, sin_q = _rope_tables(q_pos, C_ROPE // 2, C_NOPE, LANES - C_NOPE - C_ROPE)
            cos_k, sin_k = _rope_tables(k_pos, C_ROPE // 2, C_NOPE, LANES - C_NOPE - C_ROPE)
            kv_lat, kpe, qb, zs = _mla_proj(
                x, g, scale, shift, w["c_in"][j], w["c_qag"][j], w["c_kvag"][j], w["c_qb"][j],
                w["c_qg"][j], bd128, cos_q, sin_q, qscale)
            new_c.append((kv_lat, kpe))
            if past == 0:
                kv_all, kpe_all = kv_lat, kpe
            else:
                kv_all = _pad_rows(jnp.concatenate([caches[5][j], kv_lat], axis=1), tk_total)
                kpe_all = _pad_rows(jnp.concatenate([caches[6][j], kpe], axis=1), tk_total)
            kpe_slab = jnp.pad(kpe_all, ((0, 0), (0, 0), (C_NOPE, LANES - C_NOPE - C_ROPE)))
            k_all, v_all = _mla_key(kv_all, kpe_slab, w["c_kb"][j], w["c_vb"][j], w["c_kg"][j],
                                    bd128, cos_k, sin_k)
            u = _attention("mla", qb, k_all, v_all, zs, (), past=past, tk_valid=tk_valid,
                           q_norm_bound=_q_norm_bound(p["c_q_g"][j], C_NOPE + C_ROPE, qscale))
        x = _out_proj(x, u, w["out"][i], gate)

    stack = lambda rows: tuple(jnp.stack(r) for r in zip(*rows))
    return x, stack(new_a), stack(new_b), stack(new_c)


def kernel(x_prompt, x_sample, cache_a_k, cache_a_v, cache_a_logf, cache_b_k, cache_b_v, cache_c_kv, cache_c_kpe, c_prompt, c_sample, norm_g, ada_w, ada_b, out_w, a_in_w, a_f_b, a_q_g, a_k_g, b_in_w, b_q_g, b_k_g, b_lam_q1, b_lam_k1, b_lam_q2, b_lam_k2, b_sub_g, c_in_w, c_qa_g, c_kva_g, c_qb_w, c_kvb_w, c_q_g, c_k_g):
    p = dict(norm_g=norm_g, out_w=out_w, a_in_w=a_in_w, a_f_b=a_f_b, a_q_g=a_q_g, a_k_g=a_k_g,
             b_in_w=b_in_w, b_q_g=b_q_g, b_k_g=b_k_g, b_lam_q1=b_lam_q1, b_lam_k1=b_lam_k1,
             b_lam_q2=b_lam_q2, b_lam_k2=b_lam_k2, b_sub_g=b_sub_g, c_in_w=c_in_w,
             c_qa_g=c_qa_g, c_kva_g=c_kva_g, c_qb_w=c_qb_w, c_kvb_w=c_kvb_w, c_q_g=c_q_g,
             c_k_g=c_k_g)
    w = _prep_weights(p)
    bp = x_prompt.shape[0]
    mods = _ada(jnp.concatenate([c_prompt, c_sample], axis=0), ada_w, ada_b)
    caches = (cache_a_k, cache_a_v, cache_a_logf, cache_b_k, cache_b_v, cache_c_kv, cache_c_kpe)
    y_p, (a_k_p, a_v_p, a_lf_p), (b_k_p, b_v_p), (c_kv_p, c_kpe_p) = _run_trunk(
        x_prompt, mods[:, :bp], None, p, w, 0)
    y_s, (a_k_s, a_v_s, a_lf_s), (b_k_s, b_v_s), (c_kv_s, c_kpe_s) = _run_trunk(
        x_sample, mods[:, bp:], caches, p, w, cache_a_k.shape[2])
    return (y_p, y_s, a_k_p, a_v_p, a_lf_p, b_k_p, b_v_p, c_kv_p, c_kpe_p,
            a_k_s, a_v_s, a_lf_s, b_k_s, b_v_s, c_kv_s, c_kpe_s)
```

```python
import functools
import math

import numpy as np
import jax
import jax.numpy as jnp
from jax import lax
from jax.experimental import pallas as pl
from jax.experimental.pallas import tpu as pltpu

F32 = jnp.float32
BF16 = jnp.bfloat16

CHUNK = 64
CHUNK_SHIFT = 6
A_HEAD_DIM = 64
B_HEAD_DIM = 64
C_HEADS = 16
C_NOPE = 64
C_ROPE = 32
C_VDIM = 64
ROPE_THETA = 10000.0
EPS = 1e-6
NEG_INF = -1e30
LOG2E = 1.4426950408889634

FIXED_OFFSET_MAX_BOUND = 40.0
BOUND_SLACK_REL = 1.02
BOUND_SLACK_ABS = 1.0
FIXED_WIDTHS = (4, 2, 1)
LANES = 128
MXU_DIM = 256
V7X_VMEM_BYTES = 64 * 1024 * 1024
VMEM_LIMIT = V7X_VMEM_BYTES * 7 // 8

ROW_TILE = 512
ATTN_TILE = 512
DEC_KEY_TILE = 128
CUMSUM_TILE = 256


def _cparams(*sem):
    return pltpu.CompilerParams(dimension_semantics=sem, vmem_limit_bytes=VMEM_LIMIT)


def _dot(a, b):
    return jnp.dot(a, b, preferred_element_type=F32)


def _split_bf16(x):
    hi = x.astype(BF16)
    lo = (x - hi.astype(F32)).astype(BF16)
    return hi, lo


def _group_sumsq(x, bd):
    outs = []
    for c in range(x.shape[1] // MXU_DIM):
        xc = x[:, c * MXU_DIM:(c + 1) * MXU_DIM]
        outs.append(_dot((xc * xc).astype(BF16), bd))
    return outs[0] if len(outs) == 1 else jnp.concatenate(outs, axis=1)


def _modulated_norm(x, g, scale, shift):
    y = x * lax.rsqrt(jnp.mean(x * x, axis=-1, keepdims=True) + EPS) * g
    return y * (1.0 + scale) + shift


def _row_norm(x, g):
    return x * lax.rsqrt(jnp.mean(x * x, axis=-1, keepdims=True) + EPS) * g


def _swap_halves(x, half):
    lane = lax.broadcasted_iota(jnp.int32, (1, LANES), 1)
    first = (lane & (2 * half - 1)) < half
    return jnp.where(first, pltpu.roll(x, LANES - half, 1), pltpu.roll(x, half, 1))


def _rope_lanes(x, cos, sin, half):
    outs = []
    for c in range(x.shape[1] // LANES):
        xc = x[:, c * LANES:(c + 1) * LANES]
        outs.append(xc * cos + _swap_halves(xc, half) * sin)
    return outs[0] if len(outs) == 1 else jnp.concatenate(outs, axis=1)


def _silu(z):
    return z * jax.nn.sigmoid(z)


def _log_sigmoid(x):
    return jnp.minimum(x, 0.0) - jnp.log1p(jnp.exp(-jnp.abs(x)))


def _ada_kernel(c_ref, w_ref, b_ref, o_ref):
    c = c_ref[...]
    c_hi, c_lo = _split_bf16(_silu(c))
    w_hi, w_lo = _split_bf16(w_ref[...])
    o_ref[...] = _dot(c_hi, w_hi) + _dot(c_hi, w_lo) + _dot(c_lo, w_hi) + b_ref[...]


def _ada(c_all, ada_w, ada_b):
    depth, d, n3 = ada_w.shape
    rows = c_all.shape[0]
    tn = d
    return pl.pallas_call(
        _ada_kernel,
        grid=(depth, n3 // tn),
        in_specs=[
            pl.BlockSpec((rows, d), lambda l, j: (0, 0)),
            pl.BlockSpec((None, d, tn), lambda l, j: (l, 0, j)),
            pl.BlockSpec((None, 1, tn), lambda l, j: (l, 0, j)),
        ],
        out_specs=pl.BlockSpec((None, rows, tn), lambda l, j: (l, 0, j)),
        out_shape=jax.ShapeDtypeStruct((depth, rows, n3), F32),
        compiler_params=_cparams("arbitrary", "arbitrary"),
        name="ada_modulation",
    )(c_all, ada_w, ada_b.reshape(depth, 1, n3))


def _row_tile(t):
    sublane_pack = 16
    for tm in range(min(ROW_TILE, t), 0, -1):
        if t % tm == 0 and (tm % sublane_pack == 0 or tm == t):
            return tm
    raise ValueError(f"no row tile for {t} rows")


def _row_spec(tm, n):
    return pl.BlockSpec((None, tm, n), lambda b, i: (b, i, 0))


def _batch_spec(n):
    return pl.BlockSpec((None, 1, n), lambda b, i: (b, 0, 0))


def _const_spec(shape):
    return pl.BlockSpec(shape, lambda b, i: (0,) * len(shape))


def _fox_proj_kernel(x_ref, g_ref, sc_ref, sh_ref, w_ref, fb_ref, qg_ref, kg_ref, bd_ref,
                     k_out, v_out, lf_out, qb_out, kb_out, vb_out, zs_out, *, qscale):
    d = x_ref.shape[1]
    h = _modulated_norm(x_ref[...], g_ref[...], sc_ref[...], sh_ref[...]).astype(BF16)
    bd = bd_ref[...]
    inv_dim = 1.0 / A_HEAD_DIM

    q = _dot(h, w_ref[:, 0:d])
    q = q * lax.rsqrt(_group_sumsq(q, bd) * inv_dim + EPS) * qg_ref[...]
    qb_out[...] = (q * qscale).astype(BF16)

    k = _dot(h, w_ref[:, d:2 * d])
    k = k * lax.rsqrt(_group_sumsq(k, bd) * inv_dim + EPS) * kg_ref[...]
    k_out[...] = k
    kb_out[...] = k.astype(BF16)

    v = _dot(h, w_ref[:, 2 * d:3 * d])
    v_out[...] = v
    vb_out[...] = v.astype(BF16)

    zs_out[...] = _silu(_dot(h, w_ref[:, 3 * d:4 * d])).astype(BF16)

    f = _dot(h, w_ref[:, 4 * d:4 * d + LANES]) + fb_ref[...]
    lf_out[...] = _log_sigmoid(f)


def _fox_proj(x, g, scale, shift, w, fb, qg, kg, bd, qscale):
    b, t, d = x.shape
    tm = _row_tile(t)
    f32_out = jax.ShapeDtypeStruct((b, t, d), F32)
    bf_out = jax.ShapeDtypeStruct((b, t, d), BF16)
    return pl.pallas_call(
        functools.partial(_fox_proj_kernel, qscale=qscale),
        grid=(b, t // tm),
        in_specs=[_row_spec(tm, d), _const_spec((1, d)), _batch_spec(d), _batch_spec(d),
                  _const_spec(w.shape), _const_spec((1, LANES)), _const_spec((1, d)),
                  _const_spec((1, d)), _const_spec((MXU_DIM, MXU_DIM))],
        out_specs=[_row_spec(tm, d), _row_spec(tm, d), _row_spec(tm, LANES),
                   _row_spec(tm, d), _row_spec(tm, d), _row_spec(tm, d), _row_spec(tm, d)],
        out_shape=[f32_out, f32_out, jax.ShapeDtypeStruct((b, t, LANES), F32),
                   bf_out, bf_out, bf_out, bf_out],
        compiler_params=_cparams("parallel", "arbitrary"),
        name="fox_in_proj",
    )(x, g, scale, shift, w, fb, qg, kg, bd)


def _diff_proj_kernel(x_ref, g_ref, sc_ref, sh_ref, w_ref, qg_ref, kg_ref, bd_ref, cos_ref, sin_ref,
                      k_out, v_out, qb_out, kb_out, vb_out, zs_out, *, qscale):
    d = x_ref.shape[1]
    h = _modulated_norm(x_ref[...], g_ref[...], sc_ref[...], sh_ref[...]).astype(BF16)
    bd = bd_ref[...]
    cos = cos_ref[...]
    sin = sin_ref[...]
    inv_dim = 1.0 / B_HEAD_DIM
    half = B_HEAD_DIM // 2

    q = _dot(h, w_ref[:, 0:d])
    q = q * lax.rsqrt(_group_sumsq(q, bd) * inv_dim + EPS) * qg_ref[...]
    qb_out[...] = (_rope_lanes(q, cos, sin, half) * qscale).astype(BF16)

    k = _dot(h, w_ref[:, d:2 * d])
    k = k * lax.rsqrt(_group_sumsq(k, bd) * inv_dim + EPS) * kg_ref[...]
    k = _rope_lanes(k, cos, sin, half)
    k_out[...] = k
    kb_out[...] = k.astype(BF16)

    v = _dot(h, w_ref[:, 2 * d:3 * d])
    v_out[...] = v
    vb_out[...] = v.astype(BF16)

    zs_out[...] = _silu(_dot(h, w_ref[:, 3 * d:4 * d])).astype(BF16)


def _diff_proj(x, g, scale, shift, w, qg, kg, bd, cos, sin, qscale):
    b, t, d = x.shape
    tm = _row_tile(t)
    f32_out = jax.ShapeDtypeStruct((b, t, d), F32)
    bf_out = jax.ShapeDtypeStruct((b, t, d), BF16)
    table_spec = pl.BlockSpec((tm, LANES), lambda b_, i: (i, 0))
    return pl.pallas_call(
        functools.partial(_diff_proj_kernel, qscale=qscale),
        grid=(b, t // tm),
        in_specs=[_row_spec(tm, d), _const_spec((1, d)), _batch_spec(d), _batch_spec(d),
                  _const_spec(w.shape), _const_spec((1, d)), _const_spec((1, d)),
                  _const_spec((MXU_DIM, MXU_DIM)), table_spec, table_spec],
        out_specs=[_row_spec(tm, d)] * 6,
        out_shape=[f32_out, f32_out, bf_out, bf_out, bf_out, bf_out],
        compiler_params=_cparams("parallel", "arbitrary"),
        name="diff_in_proj",
    )(x, g, scale, shift, w, qg, kg, bd, cos, sin)


def _mla_proj_kernel(x_ref, g_ref, sc_ref, sh_ref, w_ref, qag_ref, kvag_ref, qbw_ref, qg_ref,
                     bd_ref, cos_ref, sin_ref,
                     kv_out, kpe_out, qb_out, zs_out, *, qscale, q_lora, kv_lora):
    d = x_ref.shape[1]
    h = _modulated_norm(x_ref[...], g_ref[...], sc_ref[...], sh_ref[...]).astype(BF16)
    n_a = q_lora + kv_lora + LANES

    a = _dot(h, w_ref[:, 0:n_a])
    qa = _row_norm(a[:, 0:q_lora], qag_ref[...]).astype(BF16)
    kv_out[...] = _row_norm(a[:, q_lora:q_lora + kv_lora], kvag_ref[...])
    kpe_out[...] = a[:, q_lora + kv_lora:q_lora + kv_lora + kpe_out.shape[1]]

    zs_out[...] = _silu(_dot(h, w_ref[:, n_a:n_a + d])).astype(BF16)

    q = _dot(qa, qbw_ref[...])
    inv_dim = 1.0 / (C_NOPE + C_ROPE)
    q = q * lax.rsqrt(_group_sumsq(q, bd_ref[...]) * inv_dim + EPS) * qg_ref[...]
    qb_out[...] = (_rope_lanes(q, cos_ref[...], sin_ref[...], C_ROPE // 2) * qscale).astype(BF16)


def _mla_proj(x, g, scale, shift, w, qag, kvag, qbw, qg, bd, cos, sin, qscale):
    b, t, d = x.shape
    tm = _row_tile(t)
    q_lora = qag.shape[1]
    kv_lora = kvag.shape[1]
    nslab = qbw.shape[1]
    table_spec = pl.BlockSpec((tm, LANES), lambda b_, i: (i, 0))
    return pl.pallas_call(
        functools.partial(_mla_proj_kernel, qscale=qscale, q_lora=q_lora, kv_lora=kv_lora),
        grid=(b, t // tm),
        in_specs=[_row_spec(tm, d), _const_spec((1, d)), _batch_spec(d), _batch_spec(d),
                  _const_spec(w.shape), _const_spec((1, q_lora)), _const_spec((1, kv_lora)),
                  _const_spec(qbw.shape), _const_spec((1, nslab)),
                  _const_spec((MXU_DIM, MXU_DIM)), table_spec, table_spec],
        out_specs=[_row_spec(tm, kv_lora), _row_spec(tm, C_ROPE), _row_spec(tm, nslab),
                   _row_spec(tm, d)],
        out_shape=[jax.ShapeDtypeStruct((b, t, kv_lora), F32),
                   jax.ShapeDtypeStruct((b, t, C_ROPE), F32),
                   jax.ShapeDtypeStruct((b, t, nslab), BF16),
                   jax.ShapeDtypeStruct((b, t, d), BF16)],
        compiler_params=_cparams("parallel", "arbitrary"),
        name="mla_in_proj",
    )(x, g, scale, shift, w, qag, kvag, qbw, qg, bd, cos, sin)


def _mla_key_kernel(kv_ref, kpe_ref, wk_ref, wv_ref, kg_ref, bd_ref, cos_ref, sin_ref,
                    kb_out, vb_out):
    kv = kv_ref[...].astype(BF16)
    vb_out[...] = _dot(kv, wv_ref[...]).astype(BF16)
    kn = _dot(kv, wk_ref[...])
    kpe = kpe_ref[...]
    k = jnp.concatenate([kn[:, c * LANES:(c + 1) * LANES] + kpe
                         for c in range(kn.shape[1] // LANES)], axis=1)
    inv_dim = 1.0 / (C_NOPE + C_ROPE)
    k = k * lax.rsqrt(_group_sumsq(k, bd_ref[...]) * inv_dim + EPS) * kg_ref[...]
    kb_out[...] = _rope_lanes(k, cos_ref[...], sin_ref[...], C_ROPE // 2).astype(BF16)


def _mla_key(kv_all, kpe_all, wk, wv, kg, bd, cos, sin):
    b, tk, kv_lora = kv_all.shape
    tm = _row_tile(tk)
    nslab = wk.shape[1]
    nv = wv.shape[1]
    table_spec = pl.BlockSpec((tm, LANES), lambda b_, i: (i, 0))
    return pl.pallas_call(
        _mla_key_kernel,
        grid=(b, tk // tm),
        in_specs=[_row_spec(tm, kv_lora), _row_spec(tm, LANES), _const_spec(wk.shape),
                  _const_spec(wv.shape), _const_spec((1, nslab)),
                  _const_spec((MXU_DIM, MXU_DIM)), table_spec, table_spec],
        out_specs=[_row_spec(tm, nslab), _row_spec(tm, nv)],
        out_shape=[jax.ShapeDtypeStruct((b, tk, nslab), BF16),
                   jax.ShapeDtypeStruct((b, tk, nv), BF16)],
        compiler_params=_cparams("parallel", "arbitrary"),
        name="mla_key_proj",
    )(kv_all, kpe_all, wk, wv, kg, bd, cos, sin)


def _out_proj_kernel(x_ref, u_ref, w_ref, gate_ref, o_ref):
    o_ref[...] = x_ref[...] + gate_ref[...] * _dot(u_ref[...], w_ref[...])


def _out_proj(x, u, w, gate):
    b, t, d = x.shape
    tm = _row_tile(t)
    return pl.pallas_call(
        _out_proj_kernel,
        grid=(b, t // tm),
        in_specs=[_row_spec(tm, d), _row_spec(tm, d), _const_spec(w.shape), _batch_spec(d)],
        out_specs=_row_spec(tm, d),
        out_shape=jax.ShapeDtypeStruct((b, t, d), F32),
        compiler_params=_cparams("parallel", "arbitrary"),
        name="out_proj_residual",
    )(x, u, w, gate)


def _split3_bf16(x):
    hi = x.astype(BF16)
    r = x - hi.astype(F32)
    mid = r.astype(BF16)
    lo = (r - mid.astype(F32)).astype(BF16)
    return hi, mid, lo


def _cumsum_kernel(x_ref, tri_ref, place_ref, cum_out, ext_out):
    tri = tri_ref[...]
    tc = tri.shape[0]
    carry = jnp.zeros((1, x_ref.shape[1]), F32)
    for r in range(x_ref.shape[0] // tc):
        rows = slice(r * tc, (r + 1) * tc)
        c = carry
        for part in _split3_bf16(x_ref[rows, :]):
            c = c + _dot(tri, part)
        cum_out[rows, :] = c
        carry = c[tc - 1:, :]
        ext = jnp.zeros((tc, ext_out.shape[1]), F32)
        for i, part in enumerate(_split3_bf16(c * LOG2E)):
            ext = ext + _dot(part, place_ref[i])
        ext_out[rows, :] = ext.astype(BF16)


def _cumsum_rows(x, heads):
    b, t, n = x.shape
    tc = next((c for c in (CUMSUM_TILE, LANES) if t % c == 0), t)
    tri = jnp.asarray(np.tril(np.ones((tc, tc), np.float32)), BF16)
    place = np.zeros((3, n, n), np.float32)
    for part in range(3):
        place[part, np.arange(heads), 3 * np.arange(heads) + part] = 1.0
    batch_spec = pl.BlockSpec((None, t, n), lambda b_: (b_, 0, 0))
    return pl.pallas_call(
        _cumsum_kernel,
        grid=(b,),
        in_specs=[batch_spec, pl.BlockSpec((tc, tc), lambda b_: (0, 0)),
                  pl.BlockSpec((3, n, n), lambda b_: (0, 0, 0))],
        out_specs=[batch_spec, batch_spec],
        out_shape=[jax.ShapeDtypeStruct((b, t, n), F32), jax.ShapeDtypeStruct((b, t, n), BF16)],
        compiler_params=_cparams("parallel"),
        name="logf_cumsum",
    )(x, tri, jnp.asarray(place, BF16))


def _attn_kernel(*refs, mode, tq, tk, nkb, past, tk_valid, lam_init):
    if mode == "fox":
        qn_ref, q_ref, k_ref, v_ref, zs_ref, kx_ref, qc_ref, o_ref = refs[:8]
    elif mode == "diff":
        qn_ref, q_ref, k_ref, v_ref, zs_ref, lam_ref, subg_ref, o_ref = refs[:8]
    else:
        qn_ref, q_ref, k_ref, v_ref, zs_ref, o_ref = refs[:6]
    vt_ref, bound_ref, m_ref, l_ref, acc_ref = refs[-5:]
    stream_lanes = LANES if mode == "mla" else LANES // 2
    vrows = acc_ref.shape[1]

    @pl.when(pl.program_id(2) == 0)
    def _():
        shift = stream_lanes.bit_length() - 1
        lane_i = lax.broadcasted_iota(jnp.int32, (LANES, LANES), 0) >> shift
        lane_j = lax.broadcasted_iota(jnp.int32, (LANES, LANES), 1) >> shift
        same_stream = jnp.where(lane_i == lane_j, 1.0, 0.0).astype(BF16)
        kmax = jnp.zeros((1, k_ref.shape[1]), F32)
        for c in range(nkb):
            vt_ref[c] = v_ref[c * tk:(c + 1) * tk, :].astype(F32).T.astype(BF16)
            kb = k_ref[c * tk:(c + 1) * tk, :].astype(F32)
            sq = (kb * kb).astype(BF16)
            norms = jnp.concatenate(
                [_dot(sq[:, g * LANES:(g + 1) * LANES], same_stream)
                 for g in range(sq.shape[1] // LANES)], axis=1)
            kmax = jnp.maximum(kmax, jnp.max(norms, axis=0, keepdims=True))
        bound = jnp.sqrt(kmax) * (qn_ref[0, 0] * BOUND_SLACK_REL) + BOUND_SLACK_ABS
        for h in range(2):
            bound_ref[h] = jnp.max(bound[:, h * stream_lanes:h * stream_lanes + 1])

    q0 = pl.program_id(2) * tq
    row = lax.broadcasted_iota(jnp.int32, (LANES, 1), 0)
    low = row < (LANES // 2)

    qt = q_ref[...].astype(F32).T
    if mode == "mla":
        rhs = (qt[0:LANES].astype(BF16), qt[LANES:2 * LANES].astype(BF16))
    else:
        rhs = (jnp.where(low, qt, 0.0).astype(BF16), jnp.where(low, 0.0, qt).astype(BF16))

    bounds = [jnp.full((1, tq), bound_ref[h], F32) for h in range(2)]
    fixed_offset = jnp.maximum(bound_ref[0], bound_ref[1]) <= FIXED_OFFSET_MAX_BOUND

    if mode == "fox":
        pair = pl.program_id(1)
        picks = []
        for h in range(2):
            first = 3 * (2 * pair + h)
            sel = jnp.logical_and(row >= first, row < first + 3)
            picks.append(jnp.broadcast_to(jnp.where(sel, -1.0, 0.0), (LANES, tq)).astype(BF16))
        rhs = tuple(jnp.concatenate([rhs[h], picks[h]], axis=0) for h in range(2))
        qcl = tuple(qc_ref[h:h + 1, :] * LOG2E for h in range(2))
    else:
        qcl = (None, None)

    m_ref[...] = jnp.full(m_ref.shape, NEG_INF, F32)
    l_ref[...] = jnp.zeros(l_ref.shape, F32)
    acc_ref[...] = jnp.zeros(acc_ref.shape, F32)

    if mode == "mla":
        zero = jnp.zeros_like(rhs[0])
        rhs_all = jnp.concatenate([jnp.concatenate([rhs[0], zero], axis=0),
                                   jnp.concatenate([zero, rhs[1]], axis=0)], axis=1)
    else:
        rhs_all = jnp.concatenate(rhs, axis=1)
    q_pos = past + q0 + lax.broadcasted_iota(jnp.int32, (1, tq), 1)
    q_pos = jnp.concatenate([q_pos, q_pos], axis=1)
    bound_all = jnp.concatenate(bounds, axis=1)
    if mode == "fox":
        qcl = jnp.concatenate(qcl, axis=1)
        offset_all = bound_all - qcl
    else:
        offset_all = bound_all
    stream_cols = (slice(0, tq), slice(tq, 2 * tq))

    def step(j, masked, fixed, width=1):
        nk = width * tk
        k0 = pl.multiple_of(j * nk, nk)

        def values(h, w, lo, n):
            rows = slice(0, LANES) if mode == "diff" else slice(h * vrows, (h + 1) * vrows)
            return vt_ref[j * width + w, rows, lo:lo + n]

        def scores(lo, n):
            ks = pl.ds(pl.multiple_of(k0 + lo, n), n)
            if mode == "fox":
                lhs = jnp.concatenate([k_ref[ks, :], kx_ref[ks, :]], axis=1)
            else:
                lhs = k_ref[ks, :]
            s = _dot(lhs, rhs_all)
            if masked:
                head = n - tk
                k_pos = k0 + lo + head + lax.broadcasted_iota(jnp.int32, (tk, 1), 0)
                if mode == "fox":
                    mask = k_pos <= q_pos
                else:
                    mask = (k_pos >> CHUNK_SHIFT) <= (q_pos >> CHUNK_SHIFT)
                    if tk_valid < nkb * tk:
                        mask = jnp.logical_and(mask, k_pos < tk_valid)
                tail = jnp.where(mask, s[head:], NEG_INF)
                s = tail if head == 0 else jnp.concatenate([s[:head], tail], axis=0)
            return s

        s = scores(0, nk)
        vt = [jnp.concatenate([values(h, w, 0, tk) for w in range(width)], axis=1)
              if width > 1 else values(h, 0, 0, tk) for h in range(2)]
        if fixed:
            p = jnp.exp2(s - offset_all)
            l_ref[...] = l_ref[...] + jnp.sum(p, axis=0, keepdims=True)
            p = p.astype(BF16)
            for h in range(2):
                acc_ref[h] = acc_ref[h] + _dot(vt[h], p[:, stream_cols[h]])
            return
        m_old = m_ref[...]
        col_max = jnp.max(s, axis=0, keepdims=True)
        if mode == "fox":
            m_new = jnp.maximum(m_old, col_max + qcl)
            p = jnp.exp2(s - (m_new - qcl))
        else:
            m_new = jnp.maximum(m_old, col_max)
            p = jnp.exp2(s - m_new)
        alpha = jnp.exp2(m_old - m_new)
        l_ref[...] = alpha * l_ref[...] + jnp.sum(p, axis=0, keepdims=True)
        m_ref[...] = m_new
        p = p.astype(BF16)
        for h in range(2):
            cols = stream_cols[h]
            acc_ref[h] = alpha[:, cols] * acc_ref[h] + _dot(vt[h], p[:, cols])

    if mode == "fox":
        n_full = (past + q0 + 1) // tk
        j_last = (past + q0 + tq - 1) // tk
    else:
        n_full = ((past + q0) // CHUNK + 1) * CHUNK // tk
        j_last = ((past + q0 + tq - 1) // CHUNK * CHUNK + CHUNK - 1) // tk
    n_full = jnp.minimum(n_full, tk_valid // tk)
    j_last = jnp.minimum(j_last, nkb - 1)

    def sweep(fixed):
        def loop(lo, hi, masked, width=1):
            def body(j, carry):
                step(j, masked, fixed, width)
                return carry
            lax.fori_loop(lo, hi, body, 0)

        if not fixed:
            loop(0, n_full, False)
            loop(n_full, j_last + 1, True)
            return
        diag_in_chain = past == 0 and tq == tk
        blocks = j_last + 1 if diag_in_chain else n_full
        start = 0
        for width in FIXED_WIDTHS:
            if width > nkb:
                continue
            trips = (blocks - start) // width
            first = start // width
            start = start + trips * width
            if diag_in_chain:
                last = jnp.logical_and(trips > 0, start == blocks)
                loop(first, first + trips - last.astype(jnp.int32), False, width)
                pl.when(last)(functools.partial(step, first + trips - 1, True, True, width))
            else:
                loop(first, first + trips, False, width)
        if not diag_in_chain:
            loop(n_full, j_last + 1, True)

    @pl.when(fixed_offset)
    def _():
        sweep(True)

    @pl.when(jnp.logical_not(fixed_offset))
    def _():
        sweep(False)

    o0 = acc_ref[0] / l_ref[:, stream_cols[0]]
    o1 = acc_ref[1] / l_ref[:, stream_cols[1]]
    if mode == "diff":
        lv = lam_ref[...]
        lam = (jnp.exp(jnp.sum(lv[0:1] * lv[1:2], axis=-1, keepdims=True))
               - jnp.exp(jnp.sum(lv[2:3] * lv[3:4], axis=-1, keepdims=True)) + lam_init)
        ot = o0 - lam * o1
        ot = ot * lax.rsqrt(jnp.mean(ot * ot, axis=0, keepdims=True) + EPS)
        ot = ot * subg_ref[...] * (1.0 - lam_init)
    else:
        ot = jnp.concatenate([o0, o1], axis=0)
    o_ref[...] = (ot.T * zs_ref[...].astype(F32)).astype(BF16)


def _q_norm_bound(gain, width, qscale):
    return math.sqrt(width) * jnp.max(jnp.abs(gain)) * qscale


def _attention(mode, q, k_all, v_all, zs, extras, *, past, tk_valid, q_norm_bound, lam_init=0.0):
    b, t, d = zs.shape
    tk_total = k_all.shape[1]
    if past == 0:
        tq = min(ATTN_TILE, t)
        tk = tq
        t_run = t
    else:
        tq = LANES
        tk = tk_total
        t_run = -(-t // tq) * tq
        q = _pad_rows(q, t_run)
        zs = _pad_rows(zs, t_run)
    nkb = tk_total // tk
    groups = d // LANES
    qw = q.shape[2] // groups

    in_specs = [
        pl.BlockSpec(memory_space=pltpu.SMEM),
        pl.BlockSpec((None, tq, qw), lambda b_, g, i: (b_, i, g)),
        pl.BlockSpec((None, tk_total, qw), lambda b_, g, i: (b_, 0, g)),
        pl.BlockSpec((None, tk_total, LANES), lambda b_, g, i: (b_, 0, g)),
        pl.BlockSpec((None, tq, LANES), lambda b_, g, i: (b_, i, g)),
    ]
    if mode == "fox":
        kx, qc = extras
        extras = (kx, jnp.pad(qc, ((0, 0), (0, 0), (0, 0), (0, t_run - t)), mode="edge"))
        in_specs += [
            pl.BlockSpec((None, tk_total, LANES), lambda b_, g, i: (b_, 0, 0)),
            pl.BlockSpec((None, None, 2, tq), lambda b_, g, i: (b_, g, 0, i)),
        ]
    elif mode == "diff":
        lam_vecs, sub_g = extras
        in_specs += [
            pl.BlockSpec(lam_vecs.shape, lambda b_, g, i: (0, 0)),
            pl.BlockSpec(sub_g.shape, lambda b_, g, i: (0, 0)),
        ]
    kern = functools.partial(_attn_kernel, mode=mode, tq=tq, tk=tk, nkb=nkb, past=past,
                             tk_valid=tk_valid, lam_init=lam_init)
    u = pl.pallas_call(
        kern,
        grid=(b, groups, t_run // tq),
        in_specs=in_specs,
        out_specs=pl.BlockSpec((None, tq, LANES), lambda b_, g, i: (b_, i, g)),
        out_shape=jax.ShapeDtypeStruct((b, t_run, d), BF16),
        scratch_shapes=[pltpu.VMEM((nkb, LANES, tk), BF16), pltpu.SMEM((2,), F32),
                        pltpu.VMEM((1, 2 * tq), F32), pltpu.VMEM((1, 2 * tq), F32),
                        pltpu.VMEM((2, LANES if mode == "diff" else LANES // 2, tq), F32)],
        compiler_params=_cparams("parallel", "parallel", "arbitrary"),
        name=mode + "_attention",
    )(jnp.reshape(q_norm_bound, (1, 1)).astype(F32), q, k_all, v_all, zs, *extras)
    return u[:, :t]


def _group_matrix(group):
    idx = np.arange(MXU_DIM) // group
    return jnp.asarray((idx[:, None] == idx[None, :]).astype(np.float32), BF16)


def _rope_tables(pos, half, lead, trail):
    inv = ROPE_THETA ** (-jnp.arange(half, dtype=F32) / half)
    ang = pos.astype(F32)[:, None] * inv
    cos = jnp.cos(ang)
    sin = jnp.sin(ang)
    reps = (LANES - lead - trail) // (2 * half)
    n = pos.shape[0]
    cos_t = jnp.concatenate([jnp.ones((n, lead), F32)] + [cos, cos] * reps
                            + [jnp.ones((n, trail), F32)], axis=1)
    sin_t = jnp.concatenate([jnp.zeros((n, lead), F32)] + [-sin, sin] * reps
                            + [jnp.zeros((n, trail), F32)], axis=1)
    return cos_t, sin_t


def _pad_rows(x, rows, axis=1):
    if x.shape[axis] == rows:
        return x
    pad = [(0, 0)] * x.ndim
    pad[axis] = (0, rows - x.shape[axis])
    return jnp.pad(x, pad)


def _key_rows(past, t):
    if past == 0:
        return t
    return -(-(past + t) // DEC_KEY_TILE) * DEC_KEY_TILE


def _prep_weights(p):
    d = p["norm_g"].shape[1]
    w = {}
    a_in = p["a_in_w"]
    heads_a = d // A_HEAD_DIM
    w["a_in"] = jnp.concatenate(
        [a_in[:, :, :3 * d], a_in[:, :, 3 * d + heads_a:],
         jnp.pad(a_in[:, :, 3 * d:3 * d + heads_a], ((0, 0), (0, 0), (0, LANES - heads_a)))],
        axis=2).astype(BF16)
    w["a_fb"] = jnp.pad(p["a_f_b"], ((0, 0), (0, LANES - heads_a)))[:, None, :]
    w["a_qg"] = jnp.tile(p["a_q_g"], (1, heads_a))[:, None, :]
    w["a_kg"] = jnp.tile(p["a_k_g"], (1, heads_a))[:, None, :]

    heads_b = d // (2 * B_HEAD_DIM)
    w["b_in"] = p["b_in_w"].astype(BF16)
    nb = p["b_q_g"].shape[0]
    w["b_qg"] = jnp.tile(p["b_q_g"].reshape(nb, 2 * B_HEAD_DIM), (1, heads_b))[:, None, :]
    w["b_kg"] = jnp.tile(p["b_k_g"].reshape(nb, 2 * B_HEAD_DIM), (1, heads_b))[:, None, :]
    w["b_lam"] = jnp.stack([p["b_lam_q1"], p["b_lam_k1"], p["b_lam_q2"], p["b_lam_k2"]], axis=1)
    w["b_subg"] = p["b_sub_g"][:, :, None]

    c_in = p["c_in_w"]
    q_lora = p["c_qa_g"].shape[1]
    kv_lora = p["c_kva_g"].shape[1]
    n0 = q_lora + kv_lora
    w["c_in"] = jnp.concatenate(
        [c_in[:, :, :n0],
         jnp.pad(c_in[:, :, n0:n0 + C_ROPE], ((0, 0), (0, 0), (0, LANES - C_ROPE))),
         c_in[:, :, n0 + C_ROPE:]], axis=2).astype(BF16)
    w["c_qag"] = p["c_qa_g"][:, None, :]
    w["c_kvag"] = p["c_kva_g"][:, None, :]
    nc = c_in.shape[0]
    hd = C_NOPE + C_ROPE
    qb = p["c_qb_w"].reshape(nc, q_lora, C_HEADS, hd)
    w["c_qb"] = jnp.pad(qb, ((0, 0), (0, 0), (0, 0), (0, LANES - hd))).reshape(
        nc, q_lora, C_HEADS * LANES).astype(BF16)
    kvb = p["c_kvb_w"].reshape(nc, kv_lora, C_HEADS, C_NOPE + C_VDIM)
    w["c_kb"] = jnp.pad(kvb[..., :C_NOPE], ((0, 0), (0, 0), (0, 0), (0, LANES - C_NOPE))).reshape(
        nc, kv_lora, C_HEADS * LANES).astype(BF16)
    w["c_vb"] = kvb[..., C_NOPE:].reshape(nc, kv_lora, C_HEADS * C_VDIM).astype(BF16)
    slab_gain = lambda g: jnp.tile(jnp.pad(g, ((0, 0), (0, LANES - hd))), (1, C_HEADS))[:, None, :]
    w["c_qg"] = slab_gain(p["c_q_g"])
    w["c_kg"] = slab_gain(p["c_k_g"])
    w["out"] = p["out_w"].astype(BF16)
    return w


def _run_trunk(x, mods, caches, p, w, past):
    b, t, d = x.shape
    depth = p["norm_g"].shape[0]
    heads_a = d // A_HEAD_DIM
    tk_valid = past + t
    tk_total = _key_rows(past, t)
    q_pos = past + jnp.arange(t)
    k_pos = jnp.arange(tk_total)
    bd64 = _group_matrix(A_HEAD_DIM)
    bd128 = _group_matrix(LANES)
    new_a, new_b, new_c = [], [], []

    def with_past(cache, new_bf):
        if past == 0:
            return new_bf
        old = cache.reshape(b, past, -1).astype(BF16)
        return _pad_rows(jnp.concatenate([old, new_bf], axis=1), tk_total)

    for i in range(depth):
        j = i // 3
        shift = mods[i][:, None, 0:d]
        scale = mods[i][:, None, d:2 * d]
        gate = mods[i][:, None, 2 * d:3 * d]
        g = p["norm_g"][i][None, :]
        if i % 3 == 0:
            qscale = A_HEAD_DIM ** -0.5 * LOG2E
            k, v, logf, qb, kb, vb, zs = _fox_proj(
                x, g, scale, shift, w["a_in"][j], w["a_fb"][j], w["a_qg"][j], w["a_kg"][j],
                bd64, qscale)
            new_a.append((k.reshape(b, t, heads_a, A_HEAD_DIM),
                          v.reshape(b, t, heads_a, A_HEAD_DIM), logf[:, :, :heads_a]))
            if past == 0:
                k_all, v_all, lf_all = kb, vb, logf
            else:
                k_all = with_past(caches[0][j], kb)
                v_all = with_past(caches[1][j], vb)
                lf_old = _pad_rows(caches[2][j], LANES, axis=2)
                lf_all = _pad_rows(jnp.concatenate([lf_old, logf], axis=1), tk_total)
            cum, kx = _cumsum_rows(lf_all, heads_a)
            qc = jnp.transpose(cum[:, past:past + t, :heads_a], (0, 2, 1)).reshape(
                b, heads_a // 2, 2, t)
            u = _attention("fox", qb, k_all, v_all, zs, (kx, qc), past=past, tk_valid=tk_valid,
                           q_norm_bound=_q_norm_bound(p["a_q_g"][j], A_HEAD_DIM, qscale))
        elif i % 3 == 1:
            qscale = B_HEAD_DIM ** -0.5 * LOG2E
            cos, sin = _rope_tables(q_pos, B_HEAD_DIM // 2, 0, 0)
            k, v, qb, kb, vb, zs = _diff_proj(
                x, g, scale, shift, w["b_in"][j], w["b_qg"][j], w["b_kg"][j], bd64, cos, sin,
                qscale)
            heads_b = d // (2 * B_HEAD_DIM)
            new_b.append((k.reshape(b, t, heads_b, 2, B_HEAD_DIM),
                          v.reshape(b, t, heads_b, 2 * B_HEAD_DIM)))
            k_all = with_past(None if past == 0 else caches[3][j], kb)
            v_all = with_past(None if past == 0 else caches[4][j], vb)
            lam_init = 0.8 - 0.6 * math.exp(-0.3 * i)
            u = _attention("diff", qb, k_all, v_all, zs, (w["b_lam"][j], w["b_subg"][j]),
                           past=past, tk_valid=tk_valid, lam_init=lam_init,
                           q_norm_bound=_q_norm_bound(p["b_q_g"][j], B_HEAD_DIM, qscale))
        else:
            qscale = (C_NOPE + C_ROPE) ** -0.5 * LOG2E
            cos_q, sin_q = _rope_tables(q_pos, C_ROPE // 2, C_NOPE, LANES - C_NOPE - C_ROPE)
            cos_k, sin_k = _rope_tables(k_pos, C_ROPE // 2, C_NOPE, LANES - C_NOPE - C_ROPE)
            kv_lat, kpe, qb, zs = _mla_proj(
                x, g, scale, shift, w["c_in"][j], w["c_qag"][j], w["c_kvag"][j], w["c_qb"][j],
                w["c_qg"][j], bd128, cos_q, sin_q, qscale)
            new_c.append((kv_lat, kpe))
            if past == 0:
                kv_all, kpe_all = kv_lat, kpe
            else:
                kv_all = _pad_rows(jnp.concatenate([caches[5][j], kv_lat], axis=1), tk_total)
                kpe_all = _pad_rows(jnp.concatenate([caches[6][j], kpe], axis=1), tk_total)
            kpe_slab = jnp.pad(kpe_all, ((0, 0), (0, 0), (C_NOPE, LANES - C_NOPE - C_ROPE)))
            k_all, v_all = _mla_key(kv_all, kpe_slab, w["c_kb"][j], w["c_vb"][j], w["c_kg"][j],
                                    bd128, cos_k, sin_k)
            u = _attention("mla", qb, k_all, v_all, zs, (), past=past, tk_valid=tk_valid,
                           q_norm_bound=_q_norm_bound(p["c_q_g"][j], C_NOPE + C_ROPE, qscale))
        x = _out_proj(x, u, w["out"][i], gate)

    stack = lambda rows: tuple(jnp.stack(r) for r in zip(*rows))
    return x, stack(new_a), stack(new_b), stack(new_c)


def kernel(x_prompt, x_sample, cache_a_k, cache_a_v, cache_a_logf, cache_b_k, cache_b_v, cache_c_kv, cache_c_kpe, c_prompt, c_sample, norm_g, ada_w, ada_b, out_w, a_in_w, a_f_b, a_q_g, a_k_g, b_in_w, b_q_g, b_k_g, b_lam_q1, b_lam_k1, b_lam_q2, b_lam_k2, b_sub_g, c_in_w, c_qa_g, c_kva_g, c_qb_w, c_kvb_w, c_q_g, c_k_g):
    p = dict(norm_g=norm_g, out_w=out_w, a_in_w=a_in_w, a_f_b=a_f_b, a_q_g=a_q_g, a_k_g=a_k_g,
             b_in_w=b_in_w, b_q_g=b_q_g, b_k_g=b_k_g, b_lam_q1=b_lam_q1, b_lam_k1=b_lam_k1,
             b_lam_q2=b_lam_q2, b_lam_k2=b_lam_k2, b_sub_g=b_sub_g, c_in_w=c_in_w,
             c_qa_g=c_qa_g, c_kva_g=c_kva_g, c_qb_w=c_qb_w, c_kvb_w=c_kvb_w, c_q_g=c_q_g,
             c_k_g=c_k_g)
    w = _prep_weights(p)
    bp = x_prompt.shape[0]
    mods = _ada(jnp.concatenate([c_prompt, c_sample], axis=0), ada_w, ada_b)
    caches = (cache_a_k, cache_a_v, cache_a_logf, cache_b_k, cache_b_v, cache_c_kv, cache_c_kpe)
    y_p, (a_k_p, a_v_p, a_lf_p), (b_k_p, b_v_p), (c_kv_p, c_kpe_p) = _run_trunk(
        x_prompt, mods[:, :bp], None, p, w, 0)
    y_s, (a_k_s, a_v_s, a_lf_s), (b_k_s, b_v_s), (c_kv_s, c_kpe_s) = _run_trunk(
        x_sample, mods[:, bp:], caches, p, w, cache_a_k.shape[2])
    return (y_p, y_s, a_k_p, a_v_p, a_lf_p, b_k_p, b_v_p, c_kv_p, c_kpe_p,
            a_k_s, a_v_s, a_lf_s, b_k_s, b_v_s, c_kv_s, c_kpe_s)
```

```python
import functools
import math

import numpy as np
import jax
import jax.numpy as jnp
from jax import lax
from jax.experimental import pallas as pl
from jax.experimental.pallas import tpu as pltpu

F32 = jnp.float32
BF16 = jnp.bfloat16

CHUNK = 64
CHUNK_SHIFT = 6
A_HEAD_DIM = 64
B_HEAD_DIM = 64
C_HEADS = 16
C_NOPE = 64
C_ROPE = 32
C_VDIM = 64
ROPE_THETA = 10000.0
EPS = 1e-6
NEG_INF = -1e30
LOG2E = 1.4426950408889634

FIXED_OFFSET_MAX_BOUND = 40.0
BOUND_SLACK_REL = 1.02
BOUND_SLACK_ABS = 1.0
FIXED_WIDTHS = (4, 2, 1)
LANES = 128
MXU_DIM = 256
V7X_VMEM_BYTES = 64 * 1024 * 1024
VMEM_LIMIT = V7X_VMEM_BYTES * 7 // 8

ROW_TILE = 512
ATTN_TILE = 512
DEC_KEY_TILE = 128
CUMSUM_TILE = 256


def _cparams(*sem):
    return pltpu.CompilerParams(dimension_semantics=sem, vmem_limit_bytes=VMEM_LIMIT)


def _dot(a, b):
    return jnp.dot(a, b, preferred_element_type=F32)


def _split_bf16(x):
    hi = x.astype(BF16)
    lo = (x - hi.astype(F32)).astype(BF16)
    return hi, lo


def _group_sumsq(x, bd):
    outs = []
    for c in range(x.shape[1] // MXU_DIM):
        xc = x[:, c * MXU_DIM:(c + 1) * MXU_DIM]
        outs.append(_dot((xc * xc).astype(BF16), bd))
    return outs[0] if len(outs) == 1 else jnp.concatenate(outs, axis=1)


def _modulated_norm(x, g, scale, shift):
    y = x * lax.rsqrt(jnp.mean(x * x, axis=-1, keepdims=True) + EPS) * g
    return y * (1.0 + scale) + shift


def _row_norm(x, g):
    return x * lax.rsqrt(jnp.mean(x * x, axis=-1, keepdims=True) + EPS) * g


def _swap_halves(x, half):
    lane = lax.broadcasted_iota(jnp.int32, (1, LANES), 1)
    first = (lane & (2 * half - 1)) < half
    return jnp.where(first, pltpu.roll(x, LANES - half, 1), pltpu.roll(x, half, 1))


def _rope_lanes(x, cos, sin, half):
    outs = []
    for c in range(x.shape[1] // LANES):
        xc = x[:, c * LANES:(c + 1) * LANES]
        outs.append(xc * cos + _swap_halves(xc, half) * sin)
    return outs[0] if len(outs) == 1 else jnp.concatenate(outs, axis=1)


def _silu(z):
    return z * jax.nn.sigmoid(z)


def _log_sigmoid(x):
    return jnp.minimum(x, 0.0) - jnp.log1p(jnp.exp(-jnp.abs(x)))


def _ada_kernel(c_ref, w_ref, b_ref, o_ref):
    c = c_ref[...]
    c_hi, c_lo = _split_bf16(_silu(c))
    w_hi, w_lo = _split_bf16(w_ref[...])
    o_ref[...] = _dot(c_hi, w_hi) + _dot(c_hi, w_lo) + _dot(c_lo, w_hi) + b_ref[...]


def _ada(c_all, ada_w, ada_b):
    depth, d, n3 = ada_w.shape
    rows = c_all.shape[0]
    tn = d
    return pl.pallas_call(
        _ada_kernel,
        grid=(depth, n3 // tn),
        in_specs=[
            pl.BlockSpec((rows, d), lambda l, j: (0, 0)),
            pl.BlockSpec((None, d, tn), lambda l, j: (l, 0, j)),
            pl.BlockSpec((None, 1, tn), lambda l, j: (l, 0, j)),
        ],
        out_specs=pl.BlockSpec((None, rows, tn), lambda l, j: (l, 0, j)),
        out_shape=jax.ShapeDtypeStruct((depth, rows, n3), F32),
        compiler_params=_cparams("arbitrary", "arbitrary"),
        name="ada_modulation",
    )(c_all, ada_w, ada_b.reshape(depth, 1, n3))


def _row_tile(t):
    sublane_pack = 16
    for tm in range(min(ROW_TILE, t), 0, -1):
        if t % tm == 0 and (tm % sublane_pack == 0 or tm == t):
            return tm
    raise ValueError(f"no row tile for {t} rows")


def _row_spec(tm, n):
    return pl.BlockSpec((None, tm, n), lambda b, i: (b, i, 0))


def _batch_spec(n):
    return pl.BlockSpec((None, 1, n), lambda b, i: (b, 0, 0))


def _const_spec(shape):
    return pl.BlockSpec(shape, lambda b, i: (0,) * len(shape))


def _fox_proj_kernel(x_ref, g_ref, sc_ref, sh_ref, w_ref, fb_ref, qg_ref, kg_ref, bd_ref,
                     k_out, v_out, lf_out, qb_out, kb_out, vb_out, zs_out, *, qscale):
    d = x_ref.shape[1]
    h = _modulated_norm(x_ref[...], g_ref[...], sc_ref[...], sh_ref[...]).astype(BF16)
    bd = bd_ref[...]
    inv_dim = 1.0 / A_HEAD_DIM

    q = _dot(h, w_ref[:, 0:d])
    q = q * lax.rsqrt(_group_sumsq(q, bd) * inv_dim + EPS) * qg_ref[...]
    qb_out[...] = (q * qscale).astype(BF16)

    k = _dot(h, w_ref[:, d:2 * d])
    k = k * lax.rsqrt(_group_sumsq(k, bd) * inv_dim + EPS) * kg_ref[...]
    k_out[...] = k
    kb_out[...] = k.astype(BF16)

    v = _dot(h, w_ref[:, 2 * d:3 * d])
    v_out[...] = v
    vb_out[...] = v.astype(BF16)

    zs_out[...] = _silu(_dot(h, w_ref[:, 3 * d:4 * d])).astype(BF16)

    f = _dot(h, w_ref[:, 4 * d:4 * d + LANES]) + fb_ref[...]
    lf_out[...] = _log_sigmoid(f)


def _fox_proj(x, g, scale, shift, w, fb, qg, kg, bd, qscale):
    b, t, d = x.shape
    tm = _row_tile(t)
    f32_out = jax.ShapeDtypeStruct((b, t, d), F32)
    bf_out = jax.ShapeDtypeStruct((b, t, d), BF16)
    return pl.pallas_call(
        functools.partial(_fox_proj_kernel, qscale=qscale),
        grid=(b, t // tm),
        in_specs=[_row_spec(tm, d), _const_spec((1, d)), _batch_spec(d), _batch_spec(d),
                  _const_spec(w.shape), _const_spec((1, LANES)), _const_spec((1, d)),
                  _const_spec((1, d)), _const_spec((MXU_DIM, MXU_DIM))],
        out_specs=[_row_spec(tm, d), _row_spec(tm, d), _row_spec(tm, LANES),
                   _row_spec(tm, d), _row_spec(tm, d), _row_spec(tm, d), _row_spec(tm, d)],
        out_shape=[f32_out, f32_out, jax.ShapeDtypeStruct((b, t, LANES), F32),
                   bf_out, bf_out, bf_out, bf_out],
        compiler_params=_cparams("parallel", "arbitrary"),
        name="fox_in_proj",
    )(x, g, scale, shift, w, fb, qg, kg, bd)


def _diff_proj_kernel(x_ref, g_ref, sc_ref, sh_ref, w_ref, qg_ref, kg_ref, bd_ref, cos_ref, sin_ref,
                      k_out, v_out, qb_out, kb_out, vb_out, zs_out, *, qscale):
    d = x_ref.shape[1]
    h = _modulated_norm(x_ref[...], g_ref[...], sc_ref[...], sh_ref[...]).astype(BF16)
    bd = bd_ref[...]
    cos = cos_ref[...]
    sin = sin_ref[...]
    inv_dim = 1.0 / B_HEAD_DIM
    half = B_HEAD_DIM // 2

    q = _dot(h, w_ref[:, 0:d])
    q = q * lax.rsqrt(_group_sumsq(q, bd) * inv_dim + EPS) * qg_ref[...]
    qb_out[...] = (_rope_lanes(q, cos, sin, half) * qscale).astype(BF16)

    k = _dot(h, w_ref[:, d:2 * d])
    k = k * lax.rsqrt(_group_sumsq(k, bd) * inv_dim + EPS) * kg_ref[...]
    k = _rope_lanes(k, cos, sin, half)
    k_out[...] = k
    kb_out[...] = k.astype(BF16)

    v = _dot(h, w_ref[:, 2 * d:3 * d])
    v_out[...] = v
    vb_out[...] = v.astype(BF16)

    zs_out[...] = _silu(_dot(h, w_ref[:, 3 * d:4 * d])).astype(BF16)


def _diff_proj(x, g, scale, shift, w, qg, kg, bd, cos, sin, qscale):
    b, t, d = x.shape
    tm = _row_tile(t)
    f32_out = jax.ShapeDtypeStruct((b, t, d), F32)
    bf_out = jax.ShapeDtypeStruct((b, t, d), BF16)
    table_spec = pl.BlockSpec((tm, LANES), lambda b_, i: (i, 0))
    return pl.pallas_call(
        functools.partial(_diff_proj_kernel, qscale=qscale),
        grid=(b, t // tm),
        in_specs=[_row_spec(tm, d), _const_spec((1, d)), _batch_spec(d), _batch_spec(d),
                  _const_spec(w.shape), _const_spec((1, d)), _const_spec((1, d)),
                  _const_spec((MXU_DIM, MXU_DIM)), table_spec, table_spec],
        out_specs=[_row_spec(tm, d)] * 6,
        out_shape=[f32_out, f32_out, bf_out, bf_out, bf_out, bf_out],
        compiler_params=_cparams("parallel", "arbitrary"),
        name="diff_in_proj",
    )(x, g, scale, shift, w, qg, kg, bd, cos, sin)


def _mla_proj_kernel(x_ref, g_ref, sc_ref, sh_ref, w_ref, qag_ref, kvag_ref, qbw_ref, qg_ref,
                     bd_ref, cos_ref, sin_ref,
                     kv_out, kpe_out, qb_out, zs_out, *, qscale, q_lora, kv_lora):
    d = x_ref.shape[1]
    h = _modulated_norm(x_ref[...], g_ref[...], sc_ref[...], sh_ref[...]).astype(BF16)
    n_a = q_lora + kv_lora + LANES

    a = _dot(h, w_ref[:, 0:n_a])
    qa = _row_norm(a[:, 0:q_lora], qag_ref[...]).astype(BF16)
    kv_out[...] = _row_norm(a[:, q_lora:q_lora + kv_lora], kvag_ref[...])
    kpe_out[...] = a[:, q_lora + kv_lora:q_lora + kv_lora + kpe_out.shape[1]]

    zs_out[...] = _silu(_dot(h, w_ref[:, n_a:n_a + d])).astype(BF16)

    q = _dot(qa, qbw_ref[...])
    inv_dim = 1.0 / (C_NOPE + C_ROPE)
    q = q * lax.rsqrt(_group_sumsq(q, bd_ref[...]) * inv_dim + EPS) * qg_ref[...]
    qb_out[...] = (_rope_lanes(q, cos_ref[...], sin_ref[...], C_ROPE // 2) * qscale).astype(BF16)


def _mla_proj(x, g, scale, shift, w, qag, kvag, qbw, qg, bd, cos, sin, qscale):
    b, t, d = x.shape
    tm = _row_tile(t)
    q_lora = qag.shape[1]
    kv_lora = kvag.shape[1]
    nslab = qbw.shape[1]
    table_spec = pl.BlockSpec((tm, LANES), lambda b_, i: (i, 0))
    return pl.pallas_call(
        functools.partial(_mla_proj_kernel, qscale=qscale, q_lora=q_lora, kv_lora=kv_lora),
        grid=(b, t // tm),
        in_specs=[_row_spec(tm, d), _const_spec((1, d)), _batch_spec(d), _batch_spec(d),
                  _const_spec(w.shape), _const_spec((1, q_lora)), _const_spec((1, kv_lora)),
                  _const_spec(qbw.shape), _const_spec((1, nslab)),
                  _const_spec((MXU_DIM, MXU_DIM)), table_spec, table_spec],
        out_specs=[_row_spec(tm, kv_lora), _row_spec(tm, C_ROPE), _row_spec(tm, nslab),
                   _row_spec(tm, d)],
        out_shape=[jax.ShapeDtypeStruct((b, t, kv_lora), F32),
                   jax.ShapeDtypeStruct((b, t, C_ROPE), F32),
                   jax.ShapeDtypeStruct((b, t, nslab), BF16),
                   jax.ShapeDtypeStruct((b, t, d), BF16)],
        compiler_params=_cparams("parallel", "arbitrary"),
        name="mla_in_proj",
    )(x, g, scale, shift, w, qag, kvag, qbw, qg, bd, cos, sin)


def _mla_key_kernel(kv_ref, kpe_ref, wk_ref, wv_ref, kg_ref, bd_ref, cos_ref, sin_ref,
                    kb_out, vb_out):
    kv = kv_ref[...].astype(BF16)
    vb_out[...] = _dot(kv, wv_ref[...]).astype(BF16)
    kn = _dot(kv, wk_ref[...])
    kpe = kpe_ref[...]
    heads = kn.shape[1] // LANES
    k = jnp.concatenate([kn[:, c * LANES:(c + 1) * LANES] + kpe for c in range(heads)], axis=1)
    inv_dim = 1.0 / (C_NOPE + C_ROPE)
    rs = lax.rsqrt(_group_sumsq(k, bd_ref[...]) * inv_dim + EPS)
    kg = kg_ref[...]
    shared = _rope_lanes(kpe * kg[:, 0:LANES], cos_ref[...], sin_ref[...], C_ROPE // 2)
    kb_out[...] = jnp.concatenate(
        [(kn[:, c * LANES:(c + 1) * LANES] * kg[:, c * LANES:(c + 1) * LANES] + shared)
         * rs[:, c * LANES:(c + 1) * LANES] for c in range(heads)], axis=1).astype(BF16)


def _mla_key(kv_all, kpe_all, wk, wv, kg, bd, cos, sin):
    b, tk, kv_lora = kv_all.shape
    tm = _row_tile(tk)
    nslab = wk.shape[1]
    nv = wv.shape[1]
    table_spec = pl.BlockSpec((tm, LANES), lambda b_, i: (i, 0))
    return pl.pallas_call(
        _mla_key_kernel,
        grid=(b, tk // tm),
        in_specs=[_row_spec(tm, kv_lora), _row_spec(tm, LANES), _const_spec(wk.shape),
                  _const_spec(wv.shape), _const_spec((1, nslab)),
                  _const_spec((MXU_DIM, MXU_DIM)), table_spec, table_spec],
        out_specs=[_row_spec(tm, nslab), _row_spec(tm, nv)],
        out_shape=[jax.ShapeDtypeStruct((b, tk, nslab), BF16),
                   jax.ShapeDtypeStruct((b, tk, nv), BF16)],
        compiler_params=_cparams("parallel", "arbitrary"),
        name="mla_key_proj",
    )(kv_all, kpe_all, wk, wv, kg, bd, cos, sin)


def _out_proj_kernel(x_ref, u_ref, w_ref, gate_ref, o_ref):
    o_ref[...] = x_ref[...] + gate_ref[...] * _dot(u_ref[...], w_ref[...])


def _out_proj(x, u, w, gate):
    b, t, d = x.shape
    tm = _row_tile(t)
    return pl.pallas_call(
        _out_proj_kernel,
        grid=(b, t // tm),
        in_specs=[_row_spec(tm, d), _row_spec(tm, d), _const_spec(w.shape), _batch_spec(d)],
        out_specs=_row_spec(tm, d),
        out_shape=jax.ShapeDtypeStruct((b, t, d), F32),
        compiler_params=_cparams("parallel", "arbitrary"),
        name="out_proj_residual",
    )(x, u, w, gate)


def _split3_bf16(x):
    hi = x.astype(BF16)
    r = x - hi.astype(F32)
    mid = r.astype(BF16)
    lo = (r - mid.astype(F32)).astype(BF16)
    return hi, mid, lo


def _cumsum_kernel(x_ref, tri_ref, place_ref, cum_out, ext_out):
    tri = tri_ref[...]
    tc = tri.shape[0]
    carry = jnp.zeros((1, x_ref.shape[1]), F32)
    for r in range(x_ref.shape[0] // tc):
        rows = slice(r * tc, (r + 1) * tc)
        c = carry
        for part in _split3_bf16(x_ref[rows, :]):
            c = c + _dot(tri, part)
        cum_out[rows, :] = c
        carry = c[tc - 1:, :]
        ext = jnp.zeros((tc, ext_out.shape[1]), F32)
        for i, part in enumerate(_split3_bf16(c * LOG2E)):
            ext = ext + _dot(part, place_ref[i])
        ext_out[rows, :] = ext.astype(BF16)


def _cumsum_rows(x, heads):
    b, t, n = x.shape
    tc = next((c for c in (CUMSUM_TILE, LANES) if t % c == 0), t)
    tri = jnp.asarray(np.tril(np.ones((tc, tc), np.float32)), BF16)
    place = np.zeros((3, n, n), np.float32)
    for part in range(3):
        place[part, np.arange(heads), 3 * np.arange(heads) + part] = 1.0
    batch_spec = pl.BlockSpec((None, t, n), lambda b_: (b_, 0, 0))
    return pl.pallas_call(
        _cumsum_kernel,
        grid=(b,),
        in_specs=[batch_spec, pl.BlockSpec((tc, tc), lambda b_: (0, 0)),
                  pl.BlockSpec((3, n, n), lambda b_: (0, 0, 0))],
        out_specs=[batch_spec, batch_spec],
        out_shape=[jax.ShapeDtypeStruct((b, t, n), F32), jax.ShapeDtypeStruct((b, t, n), BF16)],
        compiler_params=_cparams("parallel"),
        name="logf_cumsum",
    )(x, tri, jnp.asarray(place, BF16))


def _stream_lanes(mode):
    return LANES if mode == "mla" else LANES // 2


def _attn_setup(qn_ref, k_ref, v_ref, vt_ref, bound_ref, *, mode, tk, nkb):
    stream_lanes = _stream_lanes(mode)
    shift = stream_lanes.bit_length() - 1
    lane_i = lax.broadcasted_iota(jnp.int32, (LANES, LANES), 0) >> shift
    lane_j = lax.broadcasted_iota(jnp.int32, (LANES, LANES), 1) >> shift
    same_stream = jnp.where(lane_i == lane_j, 1.0, 0.0).astype(BF16)
    kmax = jnp.zeros((1, k_ref.shape[1]), F32)
    for c in range(nkb):
        vt_ref[c] = v_ref[c * tk:(c + 1) * tk, :].astype(F32).T.astype(BF16)
        kb = k_ref[c * tk:(c + 1) * tk, :].astype(F32)
        sq = (kb * kb).astype(BF16)
        norms = jnp.concatenate(
            [_dot(sq[:, g * LANES:(g + 1) * LANES], same_stream)
             for g in range(sq.shape[1] // LANES)], axis=1)
        kmax = jnp.maximum(kmax, jnp.max(norms, axis=0, keepdims=True))
    bound = jnp.sqrt(kmax) * (qn_ref[0, 0] * BOUND_SLACK_REL) + BOUND_SLACK_ABS
    for h in range(2):
        bound_ref[h] = jnp.max(bound[:, h * stream_lanes:h * stream_lanes + 1])


def _attn_kernel(*refs, mode, tq, tk, nkb, nq, past, tk_valid, lam_init):
    k_ref, v_ref = refs[2], refs[3]
    vt_ref, bound_ref = refs[-5], refs[-4]
    _attn_setup(refs[0], k_ref, v_ref, vt_ref, bound_ref, mode=mode, tk=tk, nkb=nkb)

    pair = pl.program_id(1)

    def tile(qi, carry):
        _attn_tile(qi, pair, refs, mode=mode, tq=tq, tk=tk, nkb=nkb, past=past,
                   tk_valid=tk_valid, lam_init=lam_init)
        return carry

    lax.fori_loop(0, nq, tile, 0)


def _attn_tile(qi, pair, refs, *, mode, tq, tk, nkb, past, tk_valid, lam_init):
    if mode == "fox":
        qn_ref, q_ref, k_ref, v_ref, zs_ref, kx_ref, qc_ref, o_ref = refs[:8]
    elif mode == "diff":
        qn_ref, q_ref, k_ref, v_ref, zs_ref, lam_ref, subg_ref, o_ref = refs[:8]
    else:
        qn_ref, q_ref, k_ref, v_ref, zs_ref, o_ref = refs[:6]
    vt_ref, bound_ref, m_ref, l_ref, acc_ref = refs[-5:]
    vrows = acc_ref.shape[1]

    q0 = pl.multiple_of(qi * tq, tq)
    q_rows = pl.ds(q0, tq)
    row = lax.broadcasted_iota(jnp.int32, (LANES, 1), 0)
    low = row < (LANES // 2)

    qt = q_ref[q_rows, :].astype(F32).T
    if mode == "mla":
        rhs = (qt[0:LANES].astype(BF16), qt[LANES:2 * LANES].astype(BF16))
    else:
        rhs = (jnp.where(low, qt, 0.0).astype(BF16), jnp.where(low, 0.0, qt).astype(BF16))

    bounds = [jnp.full((1, tq), bound_ref[h], F32) for h in range(2)]
    fixed_offset = jnp.maximum(bound_ref[0], bound_ref[1]) <= FIXED_OFFSET_MAX_BOUND

    if mode == "fox":
        picks = []
        for h in range(2):
            first = 3 * (2 * pair + h)
            sel = jnp.logical_and(row >= first, row < first + 3)
            picks.append(jnp.broadcast_to(jnp.where(sel, -1.0, 0.0), (LANES, tq)).astype(BF16))
        rhs = tuple(jnp.concatenate([rhs[h], picks[h]], axis=0) for h in range(2))
        qc = qc_ref[qi] * LOG2E
        qcl = tuple(qc[h:h + 1, :] for h in range(2))
    else:
        qcl = (None, None)

    m_ref[...] = jnp.full(m_ref.shape, NEG_INF, F32)
    l_ref[...] = jnp.zeros(l_ref.shape, F32)
    acc_ref[...] = jnp.zeros(acc_ref.shape, F32)

    if mode == "mla":
        zero = jnp.zeros_like(rhs[0])
        rhs_all = jnp.concatenate([jnp.concatenate([rhs[0], zero], axis=0),
                                   jnp.concatenate([zero, rhs[1]], axis=0)], axis=1)
    else:
        rhs_all = jnp.concatenate(rhs, axis=1)
    q_pos = past + q0 + lax.broadcasted_iota(jnp.int32, (1, tq), 1)
    q_pos = jnp.concatenate([q_pos, q_pos], axis=1)
    bound_all = jnp.concatenate(bounds, axis=1)
    if mode == "fox":
        qcl = jnp.concatenate(qcl, axis=1)
        offset_all = bound_all - qcl
    else:
        offset_all = bound_all
    stream_cols = (slice(0, tq), slice(tq, 2 * tq))

    def step(j, masked, fixed, width=1):
        nk = width * tk
        k0 = pl.multiple_of(j * nk, nk)

        def values(h, w, lo, n):
            rows = slice(0, LANES) if mode == "diff" else slice(h * vrows, (h + 1) * vrows)
            return vt_ref[j * width + w, rows, lo:lo + n]

        def scores(lo, n):
            ks = pl.ds(pl.multiple_of(k0 + lo, n), n)
            if mode == "fox":
                lhs = jnp.concatenate([k_ref[ks, :], kx_ref[ks, :]], axis=1)
            else:
                lhs = k_ref[ks, :]
            s = _dot(lhs, rhs_all)
            if masked:
                head = n - tk
                k_pos = k0 + lo + head + lax.broadcasted_iota(jnp.int32, (tk, 1), 0)
                if mode == "fox":
                    mask = k_pos <= q_pos
                else:
                    mask = (k_pos >> CHUNK_SHIFT) <= (q_pos >> CHUNK_SHIFT)
                    if tk_valid < nkb * tk:
                        mask = jnp.logical_and(mask, k_pos < tk_valid)
                tail = jnp.where(mask, s[head:], NEG_INF)
                s = tail if head == 0 else jnp.concatenate([s[:head], tail], axis=0)
            return s

        s = scores(0, nk)
        vt = [jnp.concatenate([values(h, w, 0, tk) for w in range(width)], axis=1)
              if width > 1 else values(h, 0, 0, tk) for h in range(2)]
        if fixed:
            p = jnp.exp2(s - offset_all)
            l_ref[...] = l_ref[...] + jnp.sum(p, axis=0, keepdims=True)
            p = p.astype(BF16)
            for h in range(2):
                acc_ref[h] = acc_ref[h] + _dot(vt[h], p[:, stream_cols[h]])
            return
        m_old = m_ref[...]
        col_max = jnp.max(s, axis=0, keepdims=True)
        if mode == "fox":
            m_new = jnp.maximum(m_old, col_max + qcl)
            p = jnp.exp2(s - (m_new - qcl))
        else:
            m_new = jnp.maximum(m_old, col_max)
            p = jnp.exp2(s - m_new)
        alpha = jnp.exp2(m_old - m_new)
        l_ref[...] = alpha * l_ref[...] + jnp.sum(p, axis=0, keepdims=True)
        m_ref[...] = m_new
        p = p.astype(BF16)
        for h in range(2):
            cols = stream_cols[h]
            acc_ref[h] = alpha[:, cols] * acc_ref[h] + _dot(vt[h], p[:, cols])

    if mode == "fox":
        n_full = (past + q0 + 1) // tk
        j_last = (past + q0 + tq - 1) // tk
    else:
        n_full = ((past + q0) // CHUNK + 1) * CHUNK // tk
        j_last = ((past + q0 + tq - 1) // CHUNK * CHUNK + CHUNK - 1) // tk
    n_full = jnp.minimum(n_full, tk_valid // tk)
    j_last = jnp.minimum(j_last, nkb - 1)

    def sweep(fixed):
        def loop(lo, hi, masked, width=1):
            def body(j, carry):
                step(j, masked, fixed, width)
                return carry
            lax.fori_loop(lo, hi, body, 0)

        if not fixed:
            loop(0, n_full, False)
            loop(n_full, j_last + 1, True)
            return
        diag_in_chain = past == 0 and tq == tk
        blocks = j_last + 1 if diag_in_chain else n_full
        start = 0
        for width in FIXED_WIDTHS:
            if width > nkb:
                continue
            trips = (blocks - start) // width
            first = start // width
            start = start + trips * width
            if diag_in_chain:
                last = jnp.logical_and(trips > 0, start == blocks)
                loop(first, first + trips - last.astype(jnp.int32), False, width)
                pl.when(last)(functools.partial(step, first + trips - 1, True, True, width))
            else:
                loop(first, first + trips, False, width)
        if not diag_in_chain:
            loop(n_full, j_last + 1, True)

    @pl.when(fixed_offset)
    def _():
        sweep(True)

    @pl.when(jnp.logical_not(fixed_offset))
    def _():
        sweep(False)

    o0 = acc_ref[0] / l_ref[:, stream_cols[0]]
    o1 = acc_ref[1] / l_ref[:, stream_cols[1]]
    if mode == "diff":
        lv = lam_ref[...]
        lam = (jnp.exp(jnp.sum(lv[0:1] * lv[1:2], axis=-1, keepdims=True))
               - jnp.exp(jnp.sum(lv[2:3] * lv[3:4], axis=-1, keepdims=True)) + lam_init)
        ot = o0 - lam * o1
        ot = ot * lax.rsqrt(jnp.mean(ot * ot, axis=0, keepdims=True) + EPS)
        ot = ot * subg_ref[...] * (1.0 - lam_init)
    else:
        ot = jnp.concatenate([o0, o1], axis=0)
    o_ref[q_rows, :] = (ot.T * zs_ref[q_rows, :].astype(F32)).astype(BF16)


def _q_norm_bound(gain, width, qscale):
    return math.sqrt(width) * jnp.max(jnp.abs(gain)) * qscale


def _attention(mode, q, k_all, v_all, zs, extras, *, past, tk_valid, q_norm_bound, lam_init=0.0):
    b, t, d = zs.shape
    tk_total = k_all.shape[1]
    if past == 0:
        tq = min(ATTN_TILE, t)
        tk = tq
        t_run = t
    else:
        tq = LANES
        tk = tk_total
        t_run = -(-t // tq) * tq
        q = _pad_rows(q, t_run)
        zs = _pad_rows(zs, t_run)
    nkb = tk_total // tk
    groups = d // LANES
    qw = q.shape[2] // groups

    nq = t_run // tq
    in_specs = [
        pl.BlockSpec(memory_space=pltpu.SMEM),
        pl.BlockSpec((None, t_run, qw), lambda b_, g: (b_, 0, g)),
        pl.BlockSpec((None, tk_total, qw), lambda b_, g: (b_, 0, g)),
        pl.BlockSpec((None, tk_total, LANES), lambda b_, g: (b_, 0, g)),
        pl.BlockSpec((None, t_run, LANES), lambda b_, g: (b_, 0, g)),
    ]
    if mode == "fox":
        kx, qc = extras
        qc = jnp.pad(qc, ((0, 0), (0, 0), (0, 0), (0, t_run - t)), mode="edge")
        qc = jnp.transpose(qc.reshape(b, groups, 2, nq, tq), (0, 1, 3, 2, 4))
        extras = (kx, qc)
        in_specs += [
            pl.BlockSpec((None, tk_total, LANES), lambda b_, g: (b_, 0, 0)),
            pl.BlockSpec((None, None, nq, 2, tq), lambda b_, g: (b_, g, 0, 0, 0)),
        ]
    elif mode == "diff":
        lam_vecs, sub_g = extras
        in_specs += [
            pl.BlockSpec(lam_vecs.shape, lambda b_, g: (0, 0)),
            pl.BlockSpec(sub_g.shape, lambda b_, g: (0, 0)),
        ]
    kern = functools.partial(_attn_kernel, mode=mode, tq=tq, tk=tk, nkb=nkb, nq=nq, past=past,
                             tk_valid=tk_valid, lam_init=lam_init)
    u = pl.pallas_call(
        kern,
        grid=(b, groups),
        in_specs=in_specs,
        out_specs=pl.BlockSpec((None, t_run, LANES), lambda b_, g: (b_, 0, g)),
        out_shape=jax.ShapeDtypeStruct((b, t_run, d), BF16),
        scratch_shapes=[pltpu.VMEM((nkb, LANES, tk), BF16), pltpu.SMEM((2,), F32),
                        pltpu.VMEM((1, 2 * tq), F32), pltpu.VMEM((1, 2 * tq), F32),
                        pltpu.VMEM((2, LANES if mode == "diff" else LANES // 2, tq), F32)],
        compiler_params=_cparams("parallel", "parallel"),
        name=mode + "_attention",
    )(jnp.reshape(q_norm_bound, (1, 1)).astype(F32), q, k_all, v_all, zs, *extras)
    return u[:, :t]


def _group_matrix(group):
    idx = np.arange(MXU_DIM) // group
    return jnp.asarray((idx[:, None] == idx[None, :]).astype(np.float32), BF16)


def _rope_tables(pos, half, lead, trail):
    inv = ROPE_THETA ** (-jnp.arange(half, dtype=F32) / half)
    ang = pos.astype(F32)[:, None] * inv
    cos = jnp.cos(ang)
    sin = jnp.sin(ang)
    reps = (LANES - lead - trail) // (2 * half)
    n = pos.shape[0]
    cos_t = jnp.concatenate([jnp.ones((n, lead), F32)] + [cos, cos] * reps
                            + [jnp.ones((n, trail), F32)], axis=1)
    sin_t = jnp.concatenate([jnp.zeros((n, lead), F32)] + [-sin, sin] * reps
                            + [jnp.zeros((n, trail), F32)], axis=1)
    return cos_t, sin_t


def _pad_rows(x, rows, axis=1):
    if x.shape[axis] == rows:
        return x
    pad = [(0, 0)] * x.ndim
    pad[axis] = (0, rows - x.shape[axis])
    return jnp.pad(x, pad)


def _key_rows(past, t):
    if past == 0:
        return t
    return -(-(past + t) // DEC_KEY_TILE) * DEC_KEY_TILE


def _prep_weights(p):
    d = p["norm_g"].shape[1]
    w = {}
    a_in = p["a_in_w"]
    heads_a = d // A_HEAD_DIM
    w["a_in"] = jnp.concatenate(
        [a_in[:, :, :3 * d], a_in[:, :, 3 * d + heads_a:],
         jnp.pad(a_in[:, :, 3 * d:3 * d + heads_a], ((0, 0), (0, 0), (0, LANES - heads_a)))],
        axis=2).astype(BF16)
    w["a_fb"] = jnp.pad(p["a_f_b"], ((0, 0), (0, LANES - heads_a)))[:, None, :]
    w["a_qg"] = jnp.tile(p["a_q_g"], (1, heads_a))[:, None, :]
    w["a_kg"] = jnp.tile(p["a_k_g"], (1, heads_a))[:, None, :]

    heads_b = d // (2 * B_HEAD_DIM)
    w["b_in"] = p["b_in_w"].astype(BF16)
    nb = p["b_q_g"].shape[0]
    w["b_qg"] = jnp.tile(p["b_q_g"].reshape(nb, 2 * B_HEAD_DIM), (1, heads_b))[:, None, :]
    w["b_kg"] = jnp.tile(p["b_k_g"].reshape(nb, 2 * B_HEAD_DIM), (1, heads_b))[:, None, :]
    w["b_lam"] = jnp.stack([p["b_lam_q1"], p["b_lam_k1"], p["b_lam_q2"], p["b_lam_k2"]], axis=1)
    w["b_subg"] = p["b_sub_g"][:, :, None]

    c_in = p["c_in_w"]
    q_lora = p["c_qa_g"].shape[1]
    kv_lora = p["c_kva_g"].shape[1]
    n0 = q_lora + kv_lora
    w["c_in"] = jnp.concatenate(
        [c_in[:, :, :n0],
         jnp.pad(c_in[:, :, n0:n0 + C_ROPE], ((0, 0), (0, 0), (0, LANES - C_ROPE))),
         c_in[:, :, n0 + C_ROPE:]], axis=2).astype(BF16)
    w["c_qag"] = p["c_qa_g"][:, None, :]
    w["c_kvag"] = p["c_kva_g"][:, None, :]
    nc = c_in.shape[0]
    hd = C_NOPE + C_ROPE
    qb = p["c_qb_w"].reshape(nc, q_lora, C_HEADS, hd)
    w["c_qb"] = jnp.pad(qb, ((0, 0), (0, 0), (0, 0), (0, LANES - hd))).reshape(
        nc, q_lora, C_HEADS * LANES).astype(BF16)
    kvb = p["c_kvb_w"].reshape(nc, kv_lora, C_HEADS, C_NOPE + C_VDIM)
    w["c_kb"] = jnp.pad(kvb[..., :C_NOPE], ((0, 0), (0, 0), (0, 0), (0, LANES - C_NOPE))).reshape(
        nc, kv_lora, C_HEADS * LANES).astype(BF16)
    w["c_vb"] = kvb[..., C_NOPE:].reshape(nc, kv_lora, C_HEADS * C_VDIM).astype(BF16)
    slab_gain = lambda g: jnp.tile(jnp.pad(g, ((0, 0), (0, LANES - hd))), (1, C_HEADS))[:, None, :]
    w["c_qg"] = slab_gain(p["c_q_g"])
    w["c_kg"] = slab_gain(p["c_k_g"])
    w["out"] = p["out_w"].astype(BF16)
    return w


def _run_trunk(x, mods, caches, p, w, past):
    b, t, d = x.shape
    depth = p["norm_g"].shape[0]
    heads_a = d // A_HEAD_DIM
    tk_valid = past + t
    tk_total = _key_rows(past, t)
    q_pos = past + jnp.arange(t)
    k_pos = jnp.arange(tk_total)
    bd64 = _group_matrix(A_HEAD_DIM)
    bd128 = _group_matrix(LANES)
    new_a, new_b, new_c = [], [], []

    def with_past(cache, new_bf):
        if past == 0:
            return new_bf
        old = cache.reshape(b, past, -1).astype(BF16)
        return _pad_rows(jnp.concatenate([old, new_bf], axis=1), tk_total)

    for i in range(depth):
        j = i // 3
        shift = mods[i][:, None, 0:d]
        scale = mods[i][:, None, d:2 * d]
        gate = mods[i][:, None, 2 * d:3 * d]
        g = p["norm_g"][i][None, :]
        if i % 3 == 0:
            qscale = A_HEAD_DIM ** -0.5 * LOG2E
            k, v, logf, qb, kb, vb, zs = _fox_proj(
                x, g, scale, shift, w["a_in"][j], w["a_fb"][j], w["a_qg"][j], w["a_kg"][j],
                bd64, qscale)
            new_a.append((k.reshape(b, t, heads_a, A_HEAD_DIM),
                          v.reshape(b, t, heads_a, A_HEAD_DIM), logf[:, :, :heads_a]))
            if past == 0:
                k_all, v_all, lf_all = kb, vb, logf
            else:
                k_all = with_past(caches[0][j], kb)
                v_all = with_past(caches[1][j], vb)
                lf_old = _pad_rows(caches[2][j], LANES, axis=2)
                lf_all = _pad_rows(jnp.concatenate([lf_old, logf], axis=1), tk_total)
            cum, kx = _cumsum_rows(lf_all, heads_a)
            qc = jnp.transpose(cum[:, past:past + t, :heads_a], (0, 2, 1)).reshape(
                b, heads_a // 2, 2, t)
            u = _attention("fox", qb, k_all, v_all, zs, (kx, qc), past=past, tk_valid=tk_valid,
                           q_norm_bound=_q_norm_bound(p["a_q_g"][j], A_HEAD_DIM, qscale))
        elif i % 3 == 1:
            qscale = B_HEAD_DIM ** -0.5 * LOG2E
            cos, sin = _rope_tables(q_pos, B_HEAD_DIM // 2, 0, 0)
            k, v, qb, kb, vb, zs = _diff_proj(
                x, g, scale, shift, w["b_in"][j], w["b_qg"][j], w["b_kg"][j], bd64, cos, sin,
                qscale)
            heads_b = d // (2 * B_HEAD_DIM)
            new_b.append((k.reshape(b, t, heads_b, 2, B_HEAD_DIM),
                          v.reshape(b, t, heads_b, 2 * B_HEAD_DIM)))
            k_all = with_past(None if past == 0 else caches[3][j], kb)
            v_all = with_past(None if past == 0 else caches[4][j], vb)
            lam_init = 0.8 - 0.6 * math.exp(-0.3 * i)
            u = _attention("diff", qb, k_all, v_all, zs, (w["b_lam"][j], w["b_subg"][j]),
                           past=past, tk_valid=tk_valid, lam_init=lam_init,
                           q_norm_bound=_q_norm_bound(p["b_q_g"][j], B_HEAD_DIM, qscale))
        else:
            qscale = (C_NOPE + C_ROPE) ** -0.5 * LOG2E
            cos_q, sin_q = _rope_tables(q_pos, C_ROPE // 2, C_NOPE, LANES - C_NOPE - C_ROPE)
            cos_k, sin_k = _rope_tables(k_pos, C_ROPE // 2, C_NOPE, LANES - C_NOPE - C_ROPE)
            kv_lat, kpe, qb, zs = _mla_proj(
                x, g, scale, shift, w["c_in"][j], w["c_qag"][j], w["c_kvag"][j], w["c_qb"][j],
                w["c_qg"][j], bd128, cos_q, sin_q, qscale)
            new_c.append((kv_lat, kpe))
            if past == 0:
                kv_all, kpe_all = kv_lat, kpe
            else:
                kv_all = _pad_rows(jnp.concatenate([caches[5][j], kv_lat], axis=1), tk_total)
                kpe_all = _pad_rows(jnp.concatenate([caches[6][j], kpe], axis=1), tk_total)
            kpe_slab = jnp.pad(kpe_all, ((0, 0), (0, 0), (C_NOPE, LANES - C_NOPE - C_ROPE)))
            k_all, v_all = _mla_key(kv_all, kpe_slab, w["c_kb"][j], w["c_vb"][j], w["c_kg"][j],
                                    bd128, cos_k, sin_k)
            u = _attention("mla", qb, k_all, v_all, zs, (), past=past, tk_valid=tk_valid,
                           q_norm_bound=_q_norm_bound(p["c_q_g"][j], C_NOPE + C_ROPE, qscale))
        x = _out_proj(x, u, w["out"][i], gate)

    stack = lambda rows: tuple(jnp.stack(r) for r in zip(*rows))
    return x, stack(new_a), stack(new_b), stack(new_c)


def kernel(x_prompt, x_sample, cache_a_k, cache_a_v, cache_a_logf, cache_b_k, cache_b_v, cache_c_kv, cache_c_kpe, c_prompt, c_sample, norm_g, ada_w, ada_b, out_w, a_in_w, a_f_b, a_q_g, a_k_g, b_in_w, b_q_g, b_k_g, b_lam_q1, b_lam_k1, b_lam_q2, b_lam_k2, b_sub_g, c_in_w, c_qa_g, c_kva_g, c_qb_w, c_kvb_w, c_q_g, c_k_g):
    p = dict(norm_g=norm_g, out_w=out_w, a_in_w=a_in_w, a_f_b=a_f_b, a_q_g=a_q_g, a_k_g=a_k_g,
             b_in_w=b_in_w, b_q_g=b_q_g, b_k_g=b_k_g, b_lam_q1=b_lam_q1, b_lam_k1=b_lam_k1,
             b_lam_q2=b_lam_q2, b_lam_k2=b_lam_k2, b_sub_g=b_sub_g, c_in_w=c_in_w,
             c_qa_g=c_qa_g, c_kva_g=c_kva_g, c_qb_w=c_qb_w, c_kvb_w=c_kvb_w, c_q_g=c_q_g,
             c_k_g=c_k_g)
    w = _prep_weights(p)
    bp = x_prompt.shape[0]
    mods = _ada(jnp.concatenate([c_prompt, c_sample], axis=0), ada_w, ada_b)
    caches = (cache_a_k, cache_a_v, cache_a_logf, cache_b_k, cache_b_v, cache_c_kv, cache_c_kpe)
    y_p, (a_k_p, a_v_p, a_lf_p), (b_k_p, b_v_p), (c_kv_p, c_kpe_p) = _run_trunk(
        x_prompt, mods[:, :bp], None, p, w, 0)
    y_s, (a_k_s, a_v_s, a_lf_s), (b_k_s, b_v_s), (c_kv_s, c_kpe_s) = _run_trunk(
        x_sample, mods[:, bp:], caches, p, w, cache_a_k.shape[2])
    return (y_p, y_s, a_k_p, a_v_p, a_lf_p, b_k_p, b_v_p, c_kv_p, c_kpe_p,
            a_k_s, a_v_s, a_lf_s, b_k_s, b_v_s, c_kv_s, c_kpe_s)
```

```python
import functools
import math

import numpy as np
import jax
import jax.numpy as jnp
from jax import lax
from jax.experimental import pallas as pl
from jax.experimental.pallas import tpu as pltpu

F32 = jnp.float32
BF16 = jnp.bfloat16

CHUNK = 64
CHUNK_SHIFT = 6
A_HEAD_DIM = 64
B_HEAD_DIM = 64
C_HEADS = 16
C_NOPE = 64
C_ROPE = 32
C_VDIM = 64
ROPE_THETA = 10000.0
EPS = 1e-6
NEG_INF = -1e30
LOG2E = 1.4426950408889634

FIXED_OFFSET_MAX_BOUND = 40.0
BOUND_SLACK_REL = 1.02
BOUND_SLACK_ABS = 1.0
FIXED_WIDTHS = (4, 2, 1)
LANES = 128
MXU_DIM = 256
V7X_VMEM_BYTES = 64 * 1024 * 1024
VMEM_LIMIT = V7X_VMEM_BYTES * 7 // 8

ROW_TILE = 512
OUT_ROW_TILE = 1024
ATTN_TILE = 512
DEC_KEY_TILE = 128
CUMSUM_TILE = 256


def _cparams(*sem):
    return pltpu.CompilerParams(dimension_semantics=sem, vmem_limit_bytes=VMEM_LIMIT)


def _dot(a, b):
    return jnp.dot(a, b, preferred_element_type=F32)


def _split_bf16(x):
    hi = x.astype(BF16)
    lo = (x - hi.astype(F32)).astype(BF16)
    return hi, lo


def _group_sumsq(x, bd):
    outs = []
    for c in range(x.shape[1] // MXU_DIM):
        xc = x[:, c * MXU_DIM:(c + 1) * MXU_DIM]
        outs.append(_dot((xc * xc).astype(BF16), bd))
    return outs[0] if len(outs) == 1 else jnp.concatenate(outs, axis=1)


def _modulated_norm(x, g, scale, shift):
    y = x * lax.rsqrt(jnp.mean(x * x, axis=-1, keepdims=True) + EPS) * g
    return y * (1.0 + scale) + shift


def _row_norm(x, g):
    return x * lax.rsqrt(jnp.mean(x * x, axis=-1, keepdims=True) + EPS) * g


def _swap_halves(x, half):
    lane = lax.broadcasted_iota(jnp.int32, (1, LANES), 1)
    first = (lane & (2 * half - 1)) < half
    return jnp.where(first, pltpu.roll(x, LANES - half, 1), pltpu.roll(x, half, 1))


def _rope_lanes(x, cos, sin, half):
    outs = []
    for c in range(x.shape[1] // LANES):
        xc = x[:, c * LANES:(c + 1) * LANES]
        outs.append(xc * cos + _swap_halves(xc, half) * sin)
    return outs[0] if len(outs) == 1 else jnp.concatenate(outs, axis=1)


def _silu(z):
    return z * jax.nn.sigmoid(z)


def _log_sigmoid(x):
    return jnp.minimum(x, 0.0) - jnp.log1p(jnp.exp(-jnp.abs(x)))


def _ada_kernel(c_ref, w_ref, b_ref, o_ref):
    c = c_ref[...]
    c_hi, c_lo = _split_bf16(_silu(c))
    w_hi, w_lo = _split_bf16(w_ref[...])
    o_ref[...] = _dot(c_hi, w_hi) + _dot(c_hi, w_lo) + _dot(c_lo, w_hi) + b_ref[...]


def _ada(c_all, ada_w, ada_b):
    depth, d, n3 = ada_w.shape
    rows = c_all.shape[0]
    tn = d
    return pl.pallas_call(
        _ada_kernel,
        grid=(depth, n3 // tn),
        in_specs=[
            pl.BlockSpec((rows, d), lambda l, j: (0, 0)),
            pl.BlockSpec((None, d, tn), lambda l, j: (l, 0, j)),
            pl.BlockSpec((None, 1, tn), lambda l, j: (l, 0, j)),
        ],
        out_specs=pl.BlockSpec((None, rows, tn), lambda l, j: (l, 0, j)),
        out_shape=jax.ShapeDtypeStruct((depth, rows, n3), F32),
        compiler_params=_cparams("arbitrary", "arbitrary"),
        name="ada_modulation",
    )(c_all, ada_w, ada_b.reshape(depth, 1, n3))


def _row_tile(t, limit=ROW_TILE):
    sublane_pack = 16
    for tm in range(min(limit, t), 0, -1):
        if t % tm == 0 and (tm % sublane_pack == 0 or tm == t):
            return tm
    raise ValueError(f"no row tile for {t} rows")


def _row_spec(tm, n):
    return pl.BlockSpec((None, tm, n), lambda b, i: (b, i, 0))


def _mod_spec(mod, n):
    _, layer, part = mod
    return pl.BlockSpec((None, None, 1, n), lambda b, i: (layer, b, 0, part))


def _const_spec(shape):
    return pl.BlockSpec(shape, lambda b, i: (0,) * len(shape))


def _fox_proj_kernel(x_ref, g_ref, sc_ref, sh_ref, w_ref, fb_ref, qg_ref, kg_ref, bd_ref,
                     k_out, v_out, lf_out, qb_out, kb_out, vb_out, zs_out, *, qscale):
    d = x_ref.shape[1]
    h = _modulated_norm(x_ref[...], g_ref[...], sc_ref[...], sh_ref[...]).astype(BF16)
    bd = bd_ref[...]
    inv_dim = 1.0 / A_HEAD_DIM

    q = _dot(h, w_ref[:, 0:d])
    q = q * lax.rsqrt(_group_sumsq(q, bd) * inv_dim + EPS) * qg_ref[...]
    qb_out[...] = (q * qscale).astype(BF16)

    k = _dot(h, w_ref[:, d:2 * d])
    k = k * lax.rsqrt(_group_sumsq(k, bd) * inv_dim + EPS) * kg_ref[...]
    k_out[...] = k
    kb_out[...] = k.astype(BF16)

    v = _dot(h, w_ref[:, 2 * d:3 * d])
    v_out[...] = v
    vb_out[...] = v.astype(BF16)

    zs_out[...] = _silu(_dot(h, w_ref[:, 3 * d:4 * d])).astype(BF16)

    f = _dot(h, w_ref[:, 4 * d:4 * d + LANES]) + fb_ref[...]
    lf_out[...] = _log_sigmoid(f)


def _fox_proj(x, g, scale, shift, w, fb, qg, kg, bd, qscale):
    b, t, d = x.shape
    tm = _row_tile(t)
    f32_out = jax.ShapeDtypeStruct((b, t, d), F32)
    bf_out = jax.ShapeDtypeStruct((b, t, d), BF16)
    return pl.pallas_call(
        functools.partial(_fox_proj_kernel, qscale=qscale),
        grid=(b, t // tm),
        in_specs=[_row_spec(tm, d), _const_spec((1, d)), _mod_spec(scale, d), _mod_spec(shift, d),
                  _const_spec(w.shape), _const_spec((1, LANES)), _const_spec((1, d)),
                  _const_spec((1, d)), _const_spec((MXU_DIM, MXU_DIM))],
        out_specs=[_row_spec(tm, d), _row_spec(tm, d), _row_spec(tm, LANES),
                   _row_spec(tm, d), _row_spec(tm, d), _row_spec(tm, d), _row_spec(tm, d)],
        out_shape=[f32_out, f32_out, jax.ShapeDtypeStruct((b, t, LANES), F32),
                   bf_out, bf_out, bf_out, bf_out],
        compiler_params=_cparams("parallel", "arbitrary"),
        name="fox_in_proj",
    )(x, g, scale[0], shift[0], w,fb, qg, kg, bd)


def _diff_proj_kernel(x_ref, g_ref, sc_ref, sh_ref, w_ref, qg_ref, kg_ref, bd_ref, cos_ref, sin_ref,
                      k_out, v_out, qb_out, kb_out, vb_out, zs_out, *, qscale):
    d = x_ref.shape[1]
    h = _modulated_norm(x_ref[...], g_ref[...], sc_ref[...], sh_ref[...]).astype(BF16)
    bd = bd_ref[...]
    cos = cos_ref[...]
    sin = sin_ref[...]
    inv_dim = 1.0 / B_HEAD_DIM
    half = B_HEAD_DIM // 2

    q = _dot(h, w_ref[:, 0:d])
    q = q * lax.rsqrt(_group_sumsq(q, bd) * inv_dim + EPS) * qg_ref[...]
    qb_out[...] = (_rope_lanes(q, cos, sin, half) * qscale).astype(BF16)

    k = _dot(h, w_ref[:, d:2 * d])
    k = k * lax.rsqrt(_group_sumsq(k, bd) * inv_dim + EPS) * kg_ref[...]
    k = _rope_lanes(k, cos, sin, half)
    k_out[...] = k
    kb_out[...] = k.astype(BF16)

    v = _dot(h, w_ref[:, 2 * d:3 * d])
    v_out[...] = v
    vb_out[...] = v.astype(BF16)

    zs_out[...] = _silu(_dot(h, w_ref[:, 3 * d:4 * d])).astype(BF16)


def _diff_proj(x, g, scale, shift, w, qg, kg, bd, cos, sin, qscale):
    b, t, d = x.shape
    tm = _row_tile(t)
    f32_out = jax.ShapeDtypeStruct((b, t, d), F32)
    bf_out = jax.ShapeDtypeStruct((b, t, d), BF16)
    table_spec = pl.BlockSpec((tm, LANES), lambda b_, i: (i, 0))
    return pl.pallas_call(
        functools.partial(_diff_proj_kernel, qscale=qscale),
        grid=(b, t // tm),
        in_specs=[_row_spec(tm, d), _const_spec((1, d)), _mod_spec(scale, d), _mod_spec(shift, d),
                  _const_spec(w.shape), _const_spec((1, d)), _const_spec((1, d)),
                  _const_spec((MXU_DIM, MXU_DIM)), table_spec, table_spec],
        out_specs=[_row_spec(tm, d)] * 6,
        out_shape=[f32_out, f32_out, bf_out, bf_out, bf_out, bf_out],
        compiler_params=_cparams("parallel", "arbitrary"),
        name="diff_in_proj",
    )(x, g, scale[0], shift[0], w,qg, kg, bd, cos, sin)


def _mla_proj_kernel(x_ref, g_ref, sc_ref, sh_ref, w_ref, qag_ref, kvag_ref, qbw_ref, qg_ref,
                     bd_ref, cos_ref, sin_ref,
                     kv_out, kpe_out, qb_out, zs_out, *, qscale, q_lora, kv_lora):
    d = x_ref.shape[1]
    h = _modulated_norm(x_ref[...], g_ref[...], sc_ref[...], sh_ref[...]).astype(BF16)
    n_a = q_lora + kv_lora + LANES

    a = _dot(h, w_ref[:, 0:n_a])
    qa = _row_norm(a[:, 0:q_lora], qag_ref[...]).astype(BF16)
    kv_out[...] = _row_norm(a[:, q_lora:q_lora + kv_lora], kvag_ref[...])
    kpe_out[...] = a[:, q_lora + kv_lora:q_lora + kv_lora + kpe_out.shape[1]]

    zs_out[...] = _silu(_dot(h, w_ref[:, n_a:n_a + d])).astype(BF16)

    q = _dot(qa, qbw_ref[...])
    inv_dim = 1.0 / (C_NOPE + C_ROPE)
    q = q * lax.rsqrt(_group_sumsq(q, bd_ref[...]) * inv_dim + EPS) * qg_ref[...]
    qb_out[...] = (_rope_lanes(q, cos_ref[...], sin_ref[...], C_ROPE // 2) * qscale).astype(BF16)


def _mla_proj(x, g, scale, shift, w, qag, kvag, qbw, qg, bd, cos, sin, qscale):
    b, t, d = x.shape
    tm = _row_tile(t)
    q_lora = qag.shape[1]
    kv_lora = kvag.shape[1]
    nslab = qbw.shape[1]
    table_spec = pl.BlockSpec((tm, LANES), lambda b_, i: (i, 0))
    return pl.pallas_call(
        functools.partial(_mla_proj_kernel, qscale=qscale, q_lora=q_lora, kv_lora=kv_lora),
        grid=(b, t // tm),
        in_specs=[_row_spec(tm, d), _const_spec((1, d)), _mod_spec(scale, d), _mod_spec(shift, d),
                  _const_spec(w.shape), _const_spec((1, q_lora)), _const_spec((1, kv_lora)),
                  _const_spec(qbw.shape), _const_spec((1, nslab)),
                  _const_spec((MXU_DIM, MXU_DIM)), table_spec, table_spec],
        out_specs=[_row_spec(tm, kv_lora), _row_spec(tm, C_ROPE), _row_spec(tm, nslab),
                   _row_spec(tm, d)],
        out_shape=[jax.ShapeDtypeStruct((b, t, kv_lora), F32),
                   jax.ShapeDtypeStruct((b, t, C_ROPE), F32),
                   jax.ShapeDtypeStruct((b, t, nslab), BF16),
                   jax.ShapeDtypeStruct((b, t, d), BF16)],
        compiler_params=_cparams("parallel", "arbitrary"),
        name="mla_in_proj",
    )(x, g, scale[0], shift[0], w,qag, kvag, qbw, qg, bd, cos, sin)


def _mla_key_kernel(kv_ref, kpe_ref, wk_ref, wv_ref, kg_ref, bd_ref, cos_ref, sin_ref,
                    kb_out, vb_out):
    kv = kv_ref[...].astype(BF16)
    vb_out[...] = _dot(kv, wv_ref[...]).astype(BF16)
    kn = _dot(kv, wk_ref[...])
    kpe = kpe_ref[...]
    heads = kn.shape[1] // LANES
    k = jnp.concatenate([kn[:, c * LANES:(c + 1) * LANES] + kpe for c in range(heads)], axis=1)
    inv_dim = 1.0 / (C_NOPE + C_ROPE)
    rs = lax.rsqrt(_group_sumsq(k, bd_ref[...]) * inv_dim + EPS)
    kg = kg_ref[...]
    shared = _rope_lanes(kpe * kg[:, 0:LANES], cos_ref[...], sin_ref[...], C_ROPE // 2)
    kb_out[...] = jnp.concatenate(
        [(kn[:, c * LANES:(c + 1) * LANES] * kg[:, c * LANES:(c + 1) * LANES] + shared)
         * rs[:, c * LANES:(c + 1) * LANES] for c in range(heads)], axis=1).astype(BF16)


def _mla_key(kv_all, kpe_all, wk, wv, kg, bd, cos, sin):
    b, tk, kv_lora = kv_all.shape
    tm = _row_tile(tk)
    nslab = wk.shape[1]
    nv = wv.shape[1]
    table_spec = pl.BlockSpec((tm, LANES), lambda b_, i: (i, 0))
    return pl.pallas_call(
        _mla_key_kernel,
        grid=(b, tk // tm),
        in_specs=[_row_spec(tm, kv_lora), _row_spec(tm, LANES), _const_spec(wk.shape),
                  _const_spec(wv.shape), _const_spec((1, nslab)),
                  _const_spec((MXU_DIM, MXU_DIM)), table_spec, table_spec],
        out_specs=[_row_spec(tm, nslab), _row_spec(tm, nv)],
        out_shape=[jax.ShapeDtypeStruct((b, tk, nslab), BF16),
                   jax.ShapeDtypeStruct((b, tk, nv), BF16)],
        compiler_params=_cparams("parallel", "arbitrary"),
        name="mla_key_proj",
    )(kv_all, kpe_all, wk, wv, kg, bd, cos, sin)


def _out_proj_kernel(x_ref, u_ref, w_ref, gate_ref, o_ref):
    o_ref[...] = x_ref[...] + gate_ref[...] * _dot(u_ref[...], w_ref[...])


def _out_proj(x, u, w, gate):
    b, t, d = x.shape
    tm = _row_tile(t, OUT_ROW_TILE)
    return pl.pallas_call(
        _out_proj_kernel,
        grid=(b, t // tm),
        in_specs=[_row_spec(tm, d), _row_spec(tm, d), _const_spec(w.shape), _mod_spec(gate, d)],
        out_specs=_row_spec(tm, d),
        out_shape=jax.ShapeDtypeStruct((b, t, d), F32),
        compiler_params=_cparams("parallel", "arbitrary"),
        name="out_proj_residual",
    )(x, u, w, gate[0])


def _split3_bf16(x):
    hi = x.astype(BF16)
    r = x - hi.astype(F32)
    mid = r.astype(BF16)
    lo = (r - mid.astype(F32)).astype(BF16)
    return hi, mid, lo


def _cumsum_kernel(x_ref, tri_ref, place_ref, cum_out, ext_out):
    tri = tri_ref[...]
    tc = tri.shape[0]
    carry = jnp.zeros((1, x_ref.shape[1]), F32)
    for r in range(x_ref.shape[0] // tc):
        rows = slice(r * tc, (r + 1) * tc)
        c = carry
        for part in _split3_bf16(x_ref[rows, :]):
            c = c + _dot(tri, part)
        cum_out[rows, :] = c
        carry = c[tc - 1:, :]
        ext = jnp.zeros((tc, ext_out.shape[1]), F32)
        for i, part in enumerate(_split3_bf16(c * LOG2E)):
            ext = ext + _dot(part, place_ref[i])
        ext_out[rows, :] = ext.astype(BF16)


def _cumsum_rows(x, heads):
    b, t, n = x.shape
    tc = next((c for c in (CUMSUM_TILE, LANES) if t % c == 0), t)
    tri = jnp.asarray(np.tril(np.ones((tc, tc), np.float32)), BF16)
    place = np.zeros((3, n, n), np.float32)
    for part in range(3):
        place[part, np.arange(heads), 3 * np.arange(heads) + part] = 1.0
    batch_spec = pl.BlockSpec((None, t, n), lambda b_: (b_, 0, 0))
    return pl.pallas_call(
        _cumsum_kernel,
        grid=(b,),
        in_specs=[batch_spec, pl.BlockSpec((tc, tc), lambda b_: (0, 0)),
                  pl.BlockSpec((3, n, n), lambda b_: (0, 0, 0))],
        out_specs=[batch_spec, batch_spec],
        out_shape=[jax.ShapeDtypeStruct((b, t, n), F32), jax.ShapeDtypeStruct((b, t, n), BF16)],
        compiler_params=_cparams("parallel"),
        name="logf_cumsum",
    )(x, tri, jnp.asarray(place, BF16))


def _stream_lanes(mode):
    return LANES if mode == "mla" else LANES // 2


def _attn_setup(qn_ref, k_ref, v_ref, vt_ref, bound_ref, *, mode, tk, nkb):
    stream_lanes = _stream_lanes(mode)
    shift = stream_lanes.bit_length() - 1
    lane_i = lax.broadcasted_iota(jnp.int32, (LANES, LANES), 0) >> shift
    lane_j = lax.broadcasted_iota(jnp.int32, (LANES, LANES), 1) >> shift
    same_stream = jnp.where(lane_i == lane_j, 1.0, 0.0).astype(BF16)
    kmax = jnp.zeros((1, k_ref.shape[1]), F32)
    for c in range(nkb):
        vt_ref[c] = v_ref[c * tk:(c + 1) * tk, :].astype(F32).T.astype(BF16)
        kb = k_ref[c * tk:(c + 1) * tk, :].astype(F32)
        sq = (kb * kb).astype(BF16)
        norms = jnp.concatenate(
            [_dot(sq[:, g * LANES:(g + 1) * LANES], same_stream)
             for g in range(sq.shape[1] // LANES)], axis=1)
        kmax = jnp.maximum(kmax, jnp.max(norms, axis=0, keepdims=True))
    bound = jnp.sqrt(kmax) * (qn_ref[0, 0] * BOUND_SLACK_REL) + BOUND_SLACK_ABS
    for h in range(2):
        bound_ref[h] = jnp.max(bound[:, h * stream_lanes:h * stream_lanes + 1])


def _attn_kernel(*refs, mode, tq, tk, nkb, nq, past, tk_valid, lam_init):
    k_ref, v_ref = refs[2], refs[3]
    vt_ref, bound_ref = refs[-5], refs[-4]
    _attn_setup(refs[0], k_ref, v_ref, vt_ref, bound_ref, mode=mode, tk=tk, nkb=nkb)

    pair = pl.program_id(1)

    def tile(qi, carry):
        _attn_tile(qi, pair, refs, mode=mode, tq=tq, tk=tk, nkb=nkb, past=past,
                   tk_valid=tk_valid, lam_init=lam_init)
        return carry

    lax.fori_loop(0, nq, tile, 0)


def _attn_tile(qi, pair, refs, *, mode, tq, tk, nkb, past, tk_valid, lam_init):
    if mode == "fox":
        qn_ref, q_ref, k_ref, v_ref, zs_ref, kx_ref, qc_ref, o_ref = refs[:8]
    elif mode == "diff":
        qn_ref, q_ref, k_ref, v_ref, zs_ref, lam_ref, subg_ref, o_ref = refs[:8]
    else:
        qn_ref, q_ref, k_ref, v_ref, zs_ref, o_ref = refs[:6]
    vt_ref, bound_ref, m_ref, l_ref, acc_ref = refs[-5:]
    vrows = acc_ref.shape[1]

    q0 = pl.multiple_of(qi * tq, tq)
    q_rows = pl.ds(q0, tq)
    row = lax.broadcasted_iota(jnp.int32, (LANES, 1), 0)
    low = row < (LANES // 2)

    qt = q_ref[q_rows, :].astype(F32).T
    if mode == "mla":
        rhs = (qt[0:LANES].astype(BF16), qt[LANES:2 * LANES].astype(BF16))
    else:
        rhs = (jnp.where(low, qt, 0.0).astype(BF16), jnp.where(low, 0.0, qt).astype(BF16))

    bounds = [jnp.full((1, tq), bound_ref[h], F32) for h in range(2)]
    fixed_offset = jnp.maximum(bound_ref[0], bound_ref[1]) <= FIXED_OFFSET_MAX_BOUND

    if mode == "fox":
        picks = []
        for h in range(2):
            first = 3 * (2 * pair + h)
            sel = jnp.logical_and(row >= first, row < first + 3)
            picks.append(jnp.broadcast_to(jnp.where(sel, -1.0, 0.0), (LANES, tq)).astype(BF16))
        rhs = tuple(jnp.concatenate([rhs[h], picks[h]], axis=0) for h in range(2))
        qc = qc_ref[qi] * LOG2E
        qcl = tuple(qc[h:h + 1, :] for h in range(2))
    else:
        qcl = (None, None)

    m_ref[...] = jnp.full(m_ref.shape, NEG_INF, F32)
    l_ref[...] = jnp.zeros(l_ref.shape, F32)
    acc_ref[...] = jnp.zeros(acc_ref.shape, F32)

    if mode == "mla":
        zero = jnp.zeros_like(rhs[0])
        rhs_all = jnp.concatenate([jnp.concatenate([rhs[0], zero], axis=0),
                                   jnp.concatenate([zero, rhs[1]], axis=0)], axis=1)
    else:
        rhs_all = jnp.concatenate(rhs, axis=1)
    q_pos = past + q0 + lax.broadcasted_iota(jnp.int32, (1, tq), 1)
    q_pos = jnp.concatenate([q_pos, q_pos], axis=1)
    bound_all = jnp.concatenate(bounds, axis=1)
    if mode == "fox":
        qcl = jnp.concatenate(qcl, axis=1)
        offset_all = bound_all - qcl
    else:
        offset_all = bound_all
    stream_cols = (slice(0, tq), slice(tq, 2 * tq))

    def step(j, masked, fixed, width=1):
        nk = width * tk
        k0 = pl.multiple_of(j * nk, nk)

        def values(h, w, lo, n):
            rows = slice(0, LANES) if mode == "diff" else slice(h * vrows, (h + 1) * vrows)
            return vt_ref[j * width + w, rows, lo:lo + n]

        def scores(lo, n):
            ks = pl.ds(pl.multiple_of(k0 + lo, n), n)
            if mode == "fox":
                lhs = jnp.concatenate([k_ref[ks, :], kx_ref[ks, :]], axis=1)
            else:
                lhs = k_ref[ks, :]
            s = _dot(lhs, rhs_all)
            if masked:
                head = n - tk
                k_pos = k0 + lo + head + lax.broadcasted_iota(jnp.int32, (tk, 1), 0)
                if mode == "fox":
                    mask = k_pos <= q_pos
                else:
                    mask = (k_pos >> CHUNK_SHIFT) <= (q_pos >> CHUNK_SHIFT)
                    if tk_valid < nkb * tk:
                        mask = jnp.logical_and(mask, k_pos < tk_valid)
                tail = jnp.where(mask, s[head:], NEG_INF)
                s = tail if head == 0 else jnp.concatenate([s[:head], tail], axis=0)
            return s

        s = scores(0, nk)
        vt = [jnp.concatenate([values(h, w, 0, tk) for w in range(width)], axis=1)
              if width > 1 else values(h, 0, 0, tk) for h in range(2)]
        if fixed:
            p = jnp.exp2(s - offset_all)
            l_ref[...] = l_ref[...] + jnp.sum(p, axis=0, keepdims=True)
            p = p.astype(BF16)
            for h in range(2):
                acc_ref[h] = acc_ref[h] + _dot(vt[h], p[:, stream_cols[h]])
            return
        m_old = m_ref[...]
        col_max = jnp.max(s, axis=0, keepdims=True)
        if mode == "fox":
            m_new = jnp.maximum(m_old, col_max + qcl)
            p = jnp.exp2(s - (m_new - qcl))
        else:
            m_new = jnp.maximum(m_old, col_max)
            p = jnp.exp2(s - m_new)
        alpha = jnp.exp2(m_old - m_new)
        l_ref[...] = alpha * l_ref[...] + jnp.sum(p, axis=0, keepdims=True)
        m_ref[...] = m_new
        p = p.astype(BF16)
        for h in range(2):
            cols = stream_cols[h]
            acc_ref[h] = alpha[:, cols] * acc_ref[h] + _dot(vt[h], p[:, cols])

    if mode == "fox":
        n_full = (past + q0 + 1) // tk
        j_last = (past + q0 + tq - 1) // tk
    else:
        n_full = ((past + q0) // CHUNK + 1) * CHUNK // tk
        j_last = ((past + q0 + tq - 1) // CHUNK * CHUNK + CHUNK - 1) // tk
    n_full = jnp.minimum(n_full, tk_valid // tk)
    j_last = jnp.minimum(j_last, nkb - 1)

    def sweep(fixed):
        def loop(lo, hi, masked, width=1):
            def body(j, carry):
                step(j, masked, fixed, width)
                return carry
            lax.fori_loop(lo, hi, body, 0)

        if not fixed:
            loop(0, n_full, False)
            loop(n_full, j_last + 1, True)
            return
        diag_in_chain = past == 0 and tq == tk
        blocks = j_last + 1 if diag_in_chain else n_full
        start = 0
        for width in FIXED_WIDTHS:
            if width > nkb:
                continue
            trips = (blocks - start) // width
            first = start // width
            start = start + trips * width
            if diag_in_chain:
                last = jnp.logical_and(trips > 0, start == blocks)
                loop(first, first + trips - last.astype(jnp.int32), False, width)
                pl.when(last)(functools.partial(step, first + trips - 1, True, True, width))
            else:
                loop(first, first + trips, False, width)
        if not diag_in_chain:
            loop(n_full, j_last + 1, True)

    @pl.when(fixed_offset)
    def _():
        sweep(True)

    @pl.when(jnp.logical_not(fixed_offset))
    def _():
        sweep(False)

    o0 = acc_ref[0] / l_ref[:, stream_cols[0]]
    o1 = acc_ref[1] / l_ref[:, stream_cols[1]]
    if mode == "diff":
        lv = lam_ref[...]
        lam = (jnp.exp(jnp.sum(lv[0:1] * lv[1:2], axis=-1, keepdims=True))
               - jnp.exp(jnp.sum(lv[2:3] * lv[3:4], axis=-1, keepdims=True)) + lam_init)
        ot = o0 - lam * o1
        ot = ot * lax.rsqrt(jnp.mean(ot * ot, axis=0, keepdims=True) + EPS)
        ot = ot * subg_ref[...] * (1.0 - lam_init)
    else:
        ot = jnp.concatenate([o0, o1], axis=0)
    o_ref[q_rows, :] = (ot.T * zs_ref[q_rows, :].astype(F32)).astype(BF16)


def _q_norm_bound(gain, width, qscale):
    return math.sqrt(width) * jnp.max(jnp.abs(gain)) * qscale


def _attention(mode, q, k_all, v_all, zs, extras, *, past, tk_valid, q_norm_bound, lam_init=0.0):
    b, t, d = zs.shape
    tk_total = k_all.shape[1]
    if past == 0:
        tq = min(ATTN_TILE, t)
        tk = tq
        t_run = t
    else:
        tq = LANES
        tk = tk_total
        t_run = -(-t // tq) * tq
        q = _pad_rows(q, t_run)
        zs = _pad_rows(zs, t_run)
    nkb = tk_total // tk
    groups = d // LANES
    qw = q.shape[2] // groups

    nq = t_run // tq
    in_specs = [
        pl.BlockSpec(memory_space=pltpu.SMEM),
        pl.BlockSpec((None, t_run, qw), lambda b_, g: (b_, 0, g)),
        pl.BlockSpec((None, tk_total, qw), lambda b_, g: (b_, 0, g)),
        pl.BlockSpec((None, tk_total, LANES), lambda b_, g: (b_, 0, g)),
        pl.BlockSpec((None, t_run, LANES), lambda b_, g: (b_, 0, g)),
    ]
    if mode == "fox":
        kx, qc = extras
        qc = jnp.pad(qc, ((0, 0), (0, 0), (0, 0), (0, t_run - t)), mode="edge")
        qc = jnp.transpose(qc.reshape(b, groups, 2, nq, tq), (0, 1, 3, 2, 4))
        extras = (kx, qc)
        in_specs += [
            pl.BlockSpec((None, tk_total, LANES), lambda b_, g: (b_, 0, 0)),
            pl.BlockSpec((None, None, nq, 2, tq), lambda b_, g: (b_, g, 0, 0, 0)),
        ]
    elif mode == "diff":
        lam_vecs, sub_g = extras
        in_specs += [
            pl.BlockSpec(lam_vecs.shape, lambda b_, g: (0, 0)),
            pl.BlockSpec(sub_g.shape, lambda b_, g: (0, 0)),
        ]
    kern = functools.partial(_attn_kernel, mode=mode, tq=tq, tk=tk, nkb=nkb, nq=nq, past=past,
                             tk_valid=tk_valid, lam_init=lam_init)
    u = pl.pallas_call(
        kern,
        grid=(b, groups),
        in_specs=in_specs,
        out_specs=pl.BlockSpec((None, t_run, LANES), lambda b_, g: (b_, 0, g)),
        out_shape=jax.ShapeDtypeStruct((b, t_run, d), BF16),
        scratch_shapes=[pltpu.VMEM((nkb, LANES, tk), BF16), pltpu.SMEM((2,), F32),
                        pltpu.VMEM((1, 2 * tq), F32), pltpu.VMEM((1, 2 * tq), F32),
                        pltpu.VMEM((2, LANES if mode == "diff" else LANES // 2, tq), F32)],
        compiler_params=_cparams("parallel", "parallel"),
        name=mode + "_attention",
    )(jnp.reshape(q_norm_bound, (1, 1)).astype(F32), q, k_all, v_all, zs, *extras)
    return u[:, :t]


def _group_matrix(group):
    idx = np.arange(MXU_DIM) // group
    return jnp.asarray((idx[:, None] == idx[None, :]).astype(np.float32), BF16)


def _rope_tables(pos, half, lead, trail):
    inv = ROPE_THETA ** (-jnp.arange(half, dtype=F32) / half)
    ang = pos.astype(F32)[:, None] * inv
    cos = jnp.cos(ang)
    sin = jnp.sin(ang)
    reps = (LANES - lead - trail) // (2 * half)
    n = pos.shape[0]
    cos_t = jnp.concatenate([jnp.ones((n, lead), F32)] + [cos, cos] * reps
                            + [jnp.ones((n, trail), F32)], axis=1)
    sin_t = jnp.concatenate([jnp.zeros((n, lead), F32)] + [-sin, sin] * reps
                            + [jnp.zeros((n, trail), F32)], axis=1)
    return cos_t, sin_t


def _pad_rows(x, rows, axis=1):
    if x.shape[axis] == rows:
        return x
    pad = [(0, 0)] * x.ndim
    pad[axis] = (0, rows - x.shape[axis])
    return jnp.pad(x, pad)


def _key_rows(past, t):
    if past == 0:
        return t
    return -(-(past + t) // DEC_KEY_TILE) * DEC_KEY_TILE


def _prep_weights(p):
    d = p["norm_g"].shape[1]
    w = {}
    a_in = p["a_in_w"]
    heads_a = d // A_HEAD_DIM
    w["a_in"] = jnp.concatenate(
        [a_in[:, :, :3 * d], a_in[:, :, 3 * d + heads_a:],
         jnp.pad(a_in[:, :, 3 * d:3 * d + heads_a], ((0, 0), (0, 0), (0, LANES - heads_a)))],
        axis=2).astype(BF16)
    w["a_fb"] = jnp.pad(p["a_f_b"], ((0, 0), (0, LANES - heads_a)))[:, None, :]
    w["a_qg"] = jnp.tile(p["a_q_g"], (1, heads_a))[:, None, :]
    w["a_kg"] = jnp.tile(p["a_k_g"], (1, heads_a))[:, None, :]

    heads_b = d // (2 * B_HEAD_DIM)
    w["b_in"] = p["b_in_w"].astype(BF16)
    nb = p["b_q_g"].shape[0]
    w["b_qg"] = jnp.tile(p["b_q_g"].reshape(nb, 2 * B_HEAD_DIM), (1, heads_b))[:, None, :]
    w["b_kg"] = jnp.tile(p["b_k_g"].reshape(nb, 2 * B_HEAD_DIM), (1, heads_b))[:, None, :]
    w["b_lam"] = jnp.stack([p["b_lam_q1"], p["b_lam_k1"], p["b_lam_q2"], p["b_lam_k2"]], axis=1)
    w["b_subg"] = p["b_sub_g"][:, :, None]

    c_in = p["c_in_w"]
    q_lora = p["c_qa_g"].shape[1]
    kv_lora = p["c_kva_g"].shape[1]
    n0 = q_lora + kv_lora
    w["c_in"] = jnp.concatenate(
        [c_in[:, :, :n0],
         jnp.pad(c_in[:, :, n0:n0 + C_ROPE], ((0, 0), (0, 0), (0, LANES - C_ROPE))),
         c_in[:, :, n0 + C_ROPE:]], axis=2).astype(BF16)
    w["c_qag"] = p["c_qa_g"][:, None, :]
    w["c_kvag"] = p["c_kva_g"][:, None, :]
    nc = c_in.shape[0]
    hd = C_NOPE + C_ROPE
    qb = p["c_qb_w"].reshape(nc, q_lora, C_HEADS, hd)
    w["c_qb"] = jnp.pad(qb, ((0, 0), (0, 0), (0, 0), (0, LANES - hd))).reshape(
        nc, q_lora, C_HEADS * LANES).astype(BF16)
    kvb = p["c_kvb_w"].reshape(nc, kv_lora, C_HEADS, C_NOPE + C_VDIM)
    w["c_kb"] = jnp.pad(kvb[..., :C_NOPE], ((0, 0), (0, 0), (0, 0), (0, LANES - C_NOPE))).reshape(
        nc, kv_lora, C_HEADS * LANES).astype(BF16)
    w["c_vb"] = kvb[..., C_NOPE:].reshape(nc, kv_lora, C_HEADS * C_VDIM).astype(BF16)
    slab_gain = lambda g: jnp.tile(jnp.pad(g, ((0, 0), (0, LANES - hd))), (1, C_HEADS))[:, None, :]
    w["c_qg"] = slab_gain(p["c_q_g"])
    w["c_kg"] = slab_gain(p["c_k_g"])
    w["out"] = p["out_w"].astype(BF16)
    return w


def _run_trunk(x, mods, caches, p, w, past):
    b, t, d = x.shape
    depth = p["norm_g"].shape[0]
    heads_a = d // A_HEAD_DIM
    tk_valid = past + t
    tk_total = _key_rows(past, t)
    q_pos = past + jnp.arange(t)
    k_pos = jnp.arange(tk_total)
    bd64 = _group_matrix(A_HEAD_DIM)
    bd128 = _group_matrix(LANES)
    new_a, new_b, new_c = [], [], []
    mods4 = mods.reshape(depth, b, 1, 3 * d)

    def with_past(cache, new_bf):
        if past == 0:
            return new_bf
        old = cache.reshape(b, past, -1).astype(BF16)
        return _pad_rows(jnp.concatenate([old, new_bf], axis=1), tk_total)

    for i in range(depth):
        j = i // 3
        shift, scale, gate = ((mods4, i, part) for part in range(3))
        g = p["norm_g"][i][None, :]
        if i % 3 == 0:
            qscale = A_HEAD_DIM ** -0.5 * LOG2E
            k, v, logf, qb, kb, vb, zs = _fox_proj(
                x, g, scale, shift, w["a_in"][j], w["a_fb"][j], w["a_qg"][j], w["a_kg"][j],
                bd64, qscale)
            new_a.append((k.reshape(b, t, heads_a, A_HEAD_DIM),
                          v.reshape(b, t, heads_a, A_HEAD_DIM), logf[:, :, :heads_a]))
            if past == 0:
                k_all, v_all, lf_all = kb, vb, logf
            else:
                k_all = with_past(caches[0][j], kb)
                v_all = with_past(caches[1][j], vb)
                lf_old = _pad_rows(caches[2][j], LANES, axis=2)
                lf_all = _pad_rows(jnp.concatenate([lf_old, logf], axis=1), tk_total)
            cum, kx = _cumsum_rows(lf_all, heads_a)
            qc = jnp.transpose(cum[:, past:past + t, :heads_a], (0, 2, 1)).reshape(
                b, heads_a // 2, 2, t)
            u = _attention("fox", qb, k_all, v_all, zs, (kx, qc), past=past, tk_valid=tk_valid,
                           q_norm_bound=_q_norm_bound(p["a_q_g"][j], A_HEAD_DIM, qscale))
        elif i % 3 == 1:
            qscale = B_HEAD_DIM ** -0.5 * LOG2E
            cos, sin = _rope_tables(q_pos, B_HEAD_DIM // 2, 0, 0)
            k, v, qb, kb, vb, zs = _diff_proj(
                x, g, scale, shift, w["b_in"][j], w["b_qg"][j], w["b_kg"][j], bd64, cos, sin,
                qscale)
            heads_b = d // (2 * B_HEAD_DIM)
            new_b.append((k.reshape(b, t, heads_b, 2, B_HEAD_DIM),
                          v.reshape(b, t, heads_b, 2 * B_HEAD_DIM)))
            k_all = with_past(None if past == 0 else caches[3][j], kb)
            v_all = with_past(None if past == 0 else caches[4][j], vb)
            lam_init = 0.8 - 0.6 * math.exp(-0.3 * i)
            u = _attention("diff", qb, k_all, v_all, zs, (w["b_lam"][j], w["b_subg"][j]),
                           past=past, tk_valid=tk_valid, lam_init=lam_init,
                           q_norm_bound=_q_norm_bound(p["b_q_g"][j], B_HEAD_DIM, qscale))
        else:
            qscale = (C_NOPE + C_ROPE) ** -0.5 * LOG2E
            cos_q, sin_q = _rope_tables(q_pos, C_ROPE // 2, C_NOPE, LANES - C_NOPE - C_ROPE)
            cos_k, sin_k = _rope_tables(k_pos, C_ROPE // 2, C_NOPE, LANES - C_NOPE - C_ROPE)
            kv_lat, kpe, qb, zs = _mla_proj(
                x, g, scale, shift, w["c_in"][j], w["c_qag"][j], w["c_kvag"][j], w["c_qb"][j],
                w["c_qg"][j], bd128, cos_q, sin_q, qscale)
            new_c.append((kv_lat, kpe))
            if past == 0:
                kv_all, kpe_all = kv_lat, kpe
            else:
                kv_all = _pad_rows(jnp.concatenate([caches[5][j], kv_lat], axis=1), tk_total)
                kpe_all = _pad_rows(jnp.concatenate([caches[6][j], kpe], axis=1), tk_total)
            kpe_slab = jnp.pad(kpe_all, ((0, 0), (0, 0), (C_NOPE, LANES - C_NOPE - C_ROPE)))
            k_all, v_all = _mla_key(kv_all, kpe_slab, w["c_kb"][j], w["c_vb"][j], w["c_kg"][j],
                                    bd128, cos_k, sin_k)
            u = _attention("mla", qb, k_all, v_all, zs, (), past=past, tk_valid=tk_valid,
                           q_norm_bound=_q_norm_bound(p["c_q_g"][j], C_NOPE + C_ROPE, qscale))
        x = _out_proj(x, u, w["out"][i], gate)

    stack = lambda rows: tuple(jnp.stack(r) for r in zip(*rows))
    return x, stack(new_a), stack(new_b), stack(new_c)


def kernel(x_prompt, x_sample, cache_a_k, cache_a_v, cache_a_logf, cache_b_k, cache_b_v, cache_c_kv, cache_c_kpe, c_prompt, c_sample, norm_g, ada_w, ada_b, out_w, a_in_w, a_f_b, a_q_g, a_k_g, b_in_w, b_q_g, b_k_g, b_lam_q1, b_lam_k1, b_lam_q2, b_lam_k2, b_sub_g, c_in_w, c_qa_g, c_kva_g, c_qb_w, c_kvb_w, c_q_g, c_k_g):
    p = dict(norm_g=norm_g, out_w=out_w, a_in_w=a_in_w, a_f_b=a_f_b, a_q_g=a_q_g, a_k_g=a_k_g,
             b_in_w=b_in_w, b_q_g=b_q_g, b_k_g=b_k_g, b_lam_q1=b_lam_q1, b_lam_k1=b_lam_k1,
             b_lam_q2=b_lam_q2, b_lam_k2=b_lam_k2, b_sub_g=b_sub_g, c_in_w=c_in_w,
             c_qa_g=c_qa_g, c_kva_g=c_kva_g, c_qb_w=c_qb_w, c_kvb_w=c_kvb_w, c_q_g=c_q_g,
             c_k_g=c_k_g)
    w = _prep_weights(p)
    bp = x_prompt.shape[0]
    mods = _ada(jnp.concatenate([c_prompt, c_sample], axis=0), ada_w, ada_b)
    caches = (cache_a_k, cache_a_v, cache_a_logf, cache_b_k, cache_b_v, cache_c_kv, cache_c_kpe)
    y_p, (a_k_p, a_v_p, a_lf_p), (b_k_p, b_v_p), (c_kv_p, c_kpe_p) = _run_trunk(
        x_prompt, mods[:, :bp], None, p, w, 0)
    y_s, (a_k_s, a_v_s, a_lf_s), (b_k_s, b_v_s), (c_kv_s, c_kpe_s) = _run_trunk(
        x_sample, mods[:, bp:], caches, p, w, cache_a_k.shape[2])
    return (y_p, y_s, a_k_p, a_v_p, a_lf_p, b_k_p, b_v_p, c_kv_p, c_kpe_p,
            a_k_s, a_v_s, a_lf_s, b_k_s, b_v_s, c_kv_s, c_kpe_s)
```

```python
import functools
import math

import numpy as np
import jax
import jax.numpy as jnp
from jax import lax
from jax.experimental import pallas as pl
from jax.experimental.pallas import tpu as pltpu

F32 = jnp.float32
BF16 = jnp.bfloat16

CHUNK = 64
CHUNK_SHIFT = 6
A_HEAD_DIM = 64
B_HEAD_DIM = 64
C_HEADS = 16
C_NOPE = 64
C_ROPE = 32
C_VDIM = 64
ROPE_THETA = 10000.0
EPS = 1e-6
NEG_INF = -1e30
LOG2E = 1.4426950408889634

FIXED_OFFSET_MAX_BOUND = 40.0
BOUND_SLACK_REL = 1.02
BOUND_SLACK_ABS = 1.0
FIXED_WIDTHS = (4, 2, 1)
LANES = 128
MXU_DIM = 256
V7X_VMEM_BYTES = 64 * 1024 * 1024
VMEM_LIMIT = V7X_VMEM_BYTES * 7 // 8

ROW_TILE = 512
OUT_ROW_TILE = 1024
ATTN_TILE = 512
DEC_KEY_TILE = 128
CUMSUM_TILE = 256


def _cparams(*sem):
    return pltpu.CompilerParams(dimension_semantics=sem, vmem_limit_bytes=VMEM_LIMIT)


def _dot(a, b):
    return jnp.dot(a, b, preferred_element_type=F32)


def _split_bf16(x):
    hi = x.astype(BF16)
    lo = (x - hi.astype(F32)).astype(BF16)
    return hi, lo


def _group_sumsq(x, bd):
    outs = []
    for c in range(x.shape[1] // MXU_DIM):
        xc = x[:, c * MXU_DIM:(c + 1) * MXU_DIM]
        outs.append(_dot((xc * xc).astype(BF16), bd))
    return outs[0] if len(outs) == 1 else jnp.concatenate(outs, axis=1)


def _modulated_norm(x, g, scale, shift):
    y = x * lax.rsqrt(jnp.mean(x * x, axis=-1, keepdims=True) + EPS) * g
    return y * (1.0 + scale) + shift


def _row_norm(x, g):
    return x * lax.rsqrt(jnp.mean(x * x, axis=-1, keepdims=True) + EPS) * g


def _swap_halves(x, half):
    lane = lax.broadcasted_iota(jnp.int32, (1, LANES), 1)
    first = (lane & (2 * half - 1)) < half
    return jnp.where(first, pltpu.roll(x, LANES - half, 1), pltpu.roll(x, half, 1))


def _rope_lanes(x, cos, sin, half):
    outs = []
    for c in range(x.shape[1] // LANES):
        xc = x[:, c * LANES:(c + 1) * LANES]
        outs.append(xc * cos + _swap_halves(xc, half) * sin)
    return outs[0] if len(outs) == 1 else jnp.concatenate(outs, axis=1)


def _silu(z):
    return z * jax.nn.sigmoid(z)


def _log_sigmoid(x):
    return jnp.minimum(x, 0.0) - jnp.log1p(jnp.exp(-jnp.abs(x)))


def _ada_kernel(c_ref, w_ref, b_ref, o_ref):
    c = c_ref[...]
    c_hi, c_lo = _split_bf16(_silu(c))
    w_hi, w_lo = _split_bf16(w_ref[...])
    o_ref[...] = _dot(c_hi, w_hi) + _dot(c_hi, w_lo) + _dot(c_lo, w_hi) + b_ref[...]


def _ada(c_all, ada_w, ada_b):
    depth, d, n3 = ada_w.shape
    rows = c_all.shape[0]
    tn = d
    return pl.pallas_call(
        _ada_kernel,
        grid=(depth, n3 // tn),
        in_specs=[
            pl.BlockSpec((rows, d), lambda l, j: (0, 0)),
            pl.BlockSpec((None, d, tn), lambda l, j: (l, 0, j)),
            pl.BlockSpec((None, 1, tn), lambda l, j: (l, 0, j)),
        ],
        out_specs=pl.BlockSpec((None, rows, tn), lambda l, j: (l, 0, j)),
        out_shape=jax.ShapeDtypeStruct((depth, rows, n3), F32),
        compiler_params=_cparams("arbitrary", "arbitrary"),
        name="ada_modulation",
    )(c_all, ada_w, ada_b.reshape(depth, 1, n3))


def _row_tile(t, limit=ROW_TILE):
    sublane_pack = 16
    for tm in range(min(limit, t), 0, -1):
        if t % tm == 0 and (tm % sublane_pack == 0 or tm == t):
            return tm
    raise ValueError(f"no row tile for {t} rows")


def _row_spec(tm, n):
    return pl.BlockSpec((None, tm, n), lambda b, i: (b, i, 0))


def _mod_spec(mod, n):
    _, layer, part = mod
    return pl.BlockSpec((None, None, 1, n), lambda b, i: (layer, b, 0, part))


def _const_spec(shape):
    return pl.BlockSpec(shape, lambda b, i: (0,) * len(shape))


def _fox_proj_kernel(x_ref, g_ref, sc_ref, sh_ref, w_ref, fb_ref, qg_ref, kg_ref, bd_ref,
                     k_out, v_out, lf_out, qb_out, kb_out, vb_out, zs_out, *, qscale):
    d = x_ref.shape[1]
    h = _modulated_norm(x_ref[...], g_ref[...], sc_ref[...], sh_ref[...]).astype(BF16)
    bd = bd_ref[...]
    inv_dim = 1.0 / A_HEAD_DIM

    q = _dot(h, w_ref[:, 0:d])
    q = q * lax.rsqrt(_group_sumsq(q, bd) * inv_dim + EPS) * qg_ref[...]
    qb_out[...] = (q * qscale).astype(BF16)

    k = _dot(h, w_ref[:, d:2 * d])
    k = k * lax.rsqrt(_group_sumsq(k, bd) * inv_dim + EPS) * kg_ref[...]
    k_out[...] = k
    kb_out[...] = k.astype(BF16)

    v = _dot(h, w_ref[:, 2 * d:3 * d])
    v_out[...] = v
    vb_out[...] = v.astype(BF16)

    zs_out[...] = _silu(_dot(h, w_ref[:, 3 * d:4 * d])).astype(BF16)

    f = _dot(h, w_ref[:, 4 * d:4 * d + LANES]) + fb_ref[...]
    lf_out[...] = _log_sigmoid(f)


def _fox_proj(x, g, scale, shift, w, fb, qg, kg, bd, qscale):
    b, t, d = x.shape
    tm = _row_tile(t)
    f32_out = jax.ShapeDtypeStruct((b, t, d), F32)
    bf_out = jax.ShapeDtypeStruct((b, t, d), BF16)
    return pl.pallas_call(
        functools.partial(_fox_proj_kernel, qscale=qscale),
        grid=(b, t // tm),
        in_specs=[_row_spec(tm, d), _const_spec((1, d)), _mod_spec(scale, d), _mod_spec(shift, d),
                  _const_spec(w.shape), _const_spec((1, LANES)), _const_spec((1, d)),
                  _const_spec((1, d)), _const_spec((MXU_DIM, MXU_DIM))],
        out_specs=[_row_spec(tm, d), _row_spec(tm, d), _row_spec(tm, LANES),
                   _row_spec(tm, d), _row_spec(tm, d), _row_spec(tm, d), _row_spec(tm, d)],
        out_shape=[f32_out, f32_out, jax.ShapeDtypeStruct((b, t, LANES), F32),
                   bf_out, bf_out, bf_out, bf_out],
        compiler_params=_cparams("parallel", "arbitrary"),
        name="fox_in_proj",
    )(x, g, scale[0], shift[0], w,fb, qg, kg, bd)


def _diff_proj_kernel(x_ref, g_ref, sc_ref, sh_ref, w_ref, qg_ref, kg_ref, bd_ref, cos_ref, sin_ref,
                      k_out, v_out, qb_out, kb_out, vb_out, zs_out, *, qscale):
    d = x_ref.shape[1]
    h = _modulated_norm(x_ref[...], g_ref[...], sc_ref[...], sh_ref[...]).astype(BF16)
    bd = bd_ref[...]
    cos = cos_ref[...]
    sin = sin_ref[...]
    inv_dim = 1.0 / B_HEAD_DIM
    half = B_HEAD_DIM // 2

    q = _dot(h, w_ref[:, 0:d])
    q = q * lax.rsqrt(_group_sumsq(q, bd) * inv_dim + EPS) * qg_ref[...]
    qb_out[...] = (_rope_lanes(q, cos, sin, half) * qscale).astype(BF16)

    k = _dot(h, w_ref[:, d:2 * d])
    k = k * lax.rsqrt(_group_sumsq(k, bd) * inv_dim + EPS) * kg_ref[...]
    k = _rope_lanes(k, cos, sin, half)
    k_out[...] = k
    kb_out[...] = k.astype(BF16)

    v = _dot(h, w_ref[:, 2 * d:3 * d])
    v_out[...] = v
    vb_out[...] = v.astype(BF16)

    zs_out[...] = _silu(_dot(h, w_ref[:, 3 * d:4 * d])).astype(BF16)


def _diff_proj(x, g, scale, shift, w, qg, kg, bd, cos, sin, qscale):
    b, t, d = x.shape
    tm = _row_tile(t)
    f32_out = jax.ShapeDtypeStruct((b, t, d), F32)
    bf_out = jax.ShapeDtypeStruct((b, t, d), BF16)
    table_spec = pl.BlockSpec((tm, LANES), lambda b_, i: (i, 0))
    return pl.pallas_call(
        functools.partial(_diff_proj_kernel, qscale=qscale),
        grid=(b, t // tm),
        in_specs=[_row_spec(tm, d), _const_spec((1, d)), _mod_spec(scale, d), _mod_spec(shift, d),
                  _const_spec(w.shape), _const_spec((1, d)), _const_spec((1, d)),
                  _const_spec((MXU_DIM, MXU_DIM)), table_spec, table_spec],
        out_specs=[_row_spec(tm, d)] * 6,
        out_shape=[f32_out, f32_out, bf_out, bf_out, bf_out, bf_out],
        compiler_params=_cparams("parallel", "arbitrary"),
        name="diff_in_proj",
    )(x, g, scale[0], shift[0], w,qg, kg, bd, cos, sin)


def _mla_proj_kernel(x_ref, g_ref, sc_ref, sh_ref, w_ref, qag_ref, kvag_ref, qbw_ref, qg_ref,
                     bd_ref, cos_ref, sin_ref,
                     kv_out, kpe_out, qb_out, zs_out, *, qscale, q_lora, kv_lora):
    d = x_ref.shape[1]
    h = _modulated_norm(x_ref[...], g_ref[...], sc_ref[...], sh_ref[...]).astype(BF16)
    n_a = q_lora + kv_lora + LANES

    a = _dot(h, w_ref[:, 0:n_a])
    qa = _row_norm(a[:, 0:q_lora], qag_ref[...]).astype(BF16)
    kv_out[...] = _row_norm(a[:, q_lora:q_lora + kv_lora], kvag_ref[...])
    kpe_out[...] = a[:, q_lora + kv_lora:q_lora + kv_lora + kpe_out.shape[1]]

    zs_out[...] = _silu(_dot(h, w_ref[:, n_a:n_a + d])).astype(BF16)

    q = _dot(qa, qbw_ref[...])
    inv_dim = 1.0 / (C_NOPE + C_ROPE)
    q = q * lax.rsqrt(_group_sumsq(q, bd_ref[...]) * inv_dim + EPS) * qg_ref[...]
    qb_out[...] = (_rope_lanes(q, cos_ref[...], sin_ref[...], C_ROPE // 2) * qscale).astype(BF16)


def _mla_proj(x, g, scale, shift, w, qag, kvag, qbw, qg, bd, cos, sin, qscale):
    b, t, d = x.shape
    tm = _row_tile(t)
    q_lora = qag.shape[1]
    kv_lora = kvag.shape[1]
    nslab = qbw.shape[1]
    table_spec = pl.BlockSpec((tm, LANES), lambda b_, i: (i, 0))
    return pl.pallas_call(
        functools.partial(_mla_proj_kernel, qscale=qscale, q_lora=q_lora, kv_lora=kv_lora),
        grid=(b, t // tm),
        in_specs=[_row_spec(tm, d), _const_spec((1, d)), _mod_spec(scale, d), _mod_spec(shift, d),
                  _const_spec(w.shape), _const_spec((1, q_lora)), _const_spec((1, kv_lora)),
                  _const_spec(qbw.shape), _const_spec((1, nslab)),
                  _const_spec((MXU_DIM, MXU_DIM)), table_spec, table_spec],
        out_specs=[_row_spec(tm, kv_lora), _row_spec(tm, C_ROPE), _row_spec(tm, nslab),
                   _row_spec(tm, d)],
        out_shape=[jax.ShapeDtypeStruct((b, t, kv_lora), F32),
                   jax.ShapeDtypeStruct((b, t, C_ROPE), F32),
                   jax.ShapeDtypeStruct((b, t, nslab), BF16),
                   jax.ShapeDtypeStruct((b, t, d), BF16)],
        compiler_params=_cparams("parallel", "arbitrary"),
        name="mla_in_proj",
    )(x, g, scale[0], shift[0], w,qag, kvag, qbw, qg, bd, cos, sin)


def _mla_key_kernel(kv_ref, kpe_ref, wk_ref, wv_ref, kg_ref, bd_ref, cos_ref, sin_ref,
                    kb_out, vb_out):
    kv = kv_ref[...].astype(BF16)
    vb_out[...] = _dot(kv, wv_ref[...]).astype(BF16)
    kn = _dot(kv, wk_ref[...])
    kpe = kpe_ref[...]
    heads = kn.shape[1] // LANES
    k = jnp.concatenate([kn[:, c * LANES:(c + 1) * LANES] + kpe for c in range(heads)], axis=1)
    inv_dim = 1.0 / (C_NOPE + C_ROPE)
    rs = lax.rsqrt(_group_sumsq(k, bd_ref[...]) * inv_dim + EPS)
    kg = kg_ref[...]
    shared = _rope_lanes(kpe * kg[:, 0:LANES], cos_ref[...], sin_ref[...], C_ROPE // 2)
    kb_out[...] = jnp.concatenate(
        [(kn[:, c * LANES:(c + 1) * LANES] * kg[:, c * LANES:(c + 1) * LANES] + shared)
         * rs[:, c * LANES:(c + 1) * LANES] for c in range(heads)], axis=1).astype(BF16)


def _mla_key(kv_all, kpe_all, wk, wv, kg, bd, cos, sin):
    b, tk, kv_lora = kv_all.shape
    tm = _row_tile(tk)
    nslab = wk.shape[1]
    nv = wv.shape[1]
    table_spec = pl.BlockSpec((tm, LANES), lambda b_, i: (i, 0))
    return pl.pallas_call(
        _mla_key_kernel,
        grid=(b, tk // tm),
        in_specs=[_row_spec(tm, kv_lora), _row_spec(tm, LANES), _const_spec(wk.shape),
                  _const_spec(wv.shape), _const_spec((1, nslab)),
                  _const_spec((MXU_DIM, MXU_DIM)), table_spec, table_spec],
        out_specs=[_row_spec(tm, nslab), _row_spec(tm, nv)],
        out_shape=[jax.ShapeDtypeStruct((b, tk, nslab), BF16),
                   jax.ShapeDtypeStruct((b, tk, nv), BF16)],
        compiler_params=_cparams("parallel", "arbitrary"),
        name="mla_key_proj",
    )(kv_all, kpe_all, wk, wv, kg, bd, cos, sin)


def _out_proj_kernel(x_ref, u_ref, w_ref, gate_ref, o_ref):
    o_ref[...] = x_ref[...] + gate_ref[...] * _dot(u_ref[...], w_ref[...])


def _out_proj(x, u, w, gate):
    b, t, d = x.shape
    tm = _row_tile(t, OUT_ROW_TILE)
    return pl.pallas_call(
        _out_proj_kernel,
        grid=(b, t // tm),
        in_specs=[_row_spec(tm, d), _row_spec(tm, d), _const_spec(w.shape), _mod_spec(gate, d)],
        out_specs=_row_spec(tm, d),
        out_shape=jax.ShapeDtypeStruct((b, t, d), F32),
        compiler_params=_cparams("parallel", "arbitrary"),
        name="out_proj_residual",
    )(x, u, w, gate[0])


def _split3_bf16(x):
    hi = x.astype(BF16)
    r = x - hi.astype(F32)
    mid = r.astype(BF16)
    lo = (r - mid.astype(F32)).astype(BF16)
    return hi, mid, lo


def _cumsum_kernel(x_ref, tri_ref, place_ref, cum_out, ext_out):
    tri = tri_ref[...]
    tc = tri.shape[0]
    carry = jnp.zeros((1, x_ref.shape[1]), F32)
    for r in range(x_ref.shape[0] // tc):
        rows = slice(r * tc, (r + 1) * tc)
        c = carry
        for part in _split3_bf16(x_ref[rows, :]):
            c = c + _dot(tri, part)
        cum_out[rows, :] = c
        carry = c[tc - 1:, :]
        ext = jnp.zeros((tc, ext_out.shape[1]), F32)
        for i, part in enumerate(_split3_bf16(c * LOG2E)):
            ext = ext + _dot(part, place_ref[i])
        ext_out[rows, :] = ext.astype(BF16)


def _cumsum_rows(x, heads):
    b, t, n = x.shape
    tc = next((c for c in (CUMSUM_TILE, LANES) if t % c == 0), t)
    tri = jnp.asarray(np.tril(np.ones((tc, tc), np.float32)), BF16)
    place = np.zeros((3, n, n), np.float32)
    for part in range(3):
        place[part, np.arange(heads), 3 * np.arange(heads) + part] = 1.0
    batch_spec = pl.BlockSpec((None, t, n), lambda b_: (b_, 0, 0))
    return pl.pallas_call(
        _cumsum_kernel,
        grid=(b,),
        in_specs=[batch_spec, pl.BlockSpec((tc, tc), lambda b_: (0, 0)),
                  pl.BlockSpec((3, n, n), lambda b_: (0, 0, 0))],
        out_specs=[batch_spec, batch_spec],
        out_shape=[jax.ShapeDtypeStruct((b, t, n), F32), jax.ShapeDtypeStruct((b, t, n), BF16)],
        compiler_params=_cparams("parallel"),
        name="logf_cumsum",
    )(x, tri, jnp.asarray(place, BF16))


def _stream_lanes(mode):
    return LANES if mode == "mla" else LANES // 2


def _attn_setup(qn_ref, k_ref, v_ref, vt_ref, bound_ref, *, mode, tk, nkb, measure_keys):
    for c in range(nkb):
        vt_ref[c] = v_ref[c * tk:(c + 1) * tk, :].astype(F32).T.astype(BF16)
    if not measure_keys:
        for h in range(2):
            bound_ref[h] = qn_ref[0, 0] * qn_ref[0, 1] * BOUND_SLACK_REL + BOUND_SLACK_ABS
        return
    stream_lanes = _stream_lanes(mode)
    shift = stream_lanes.bit_length() - 1
    lane_i = lax.broadcasted_iota(jnp.int32, (LANES, LANES), 0) >> shift
    lane_j = lax.broadcasted_iota(jnp.int32, (LANES, LANES), 1) >> shift
    same_stream = jnp.where(lane_i == lane_j, 1.0, 0.0).astype(BF16)
    kmax = jnp.zeros((1, k_ref.shape[1]), F32)
    for c in range(nkb):
        kb = k_ref[c * tk:(c + 1) * tk, :].astype(F32)
        sq = (kb * kb).astype(BF16)
        norms = jnp.concatenate(
            [_dot(sq[:, g * LANES:(g + 1) * LANES], same_stream)
             for g in range(sq.shape[1] // LANES)], axis=1)
        kmax = jnp.maximum(kmax, jnp.max(norms, axis=0, keepdims=True))
    bound = jnp.sqrt(kmax) * (qn_ref[0, 0] * BOUND_SLACK_REL) + BOUND_SLACK_ABS
    for h in range(2):
        bound_ref[h] = jnp.max(bound[:, h * stream_lanes:h * stream_lanes + 1])


def _attn_kernel(*refs, mode, tq, tk, nkb, nq, past, tk_valid, lam_init):
    k_ref, v_ref = refs[2], refs[3]
    vt_ref, bound_ref = refs[-5], refs[-4]
    _attn_setup(refs[0], k_ref, v_ref, vt_ref, bound_ref, mode=mode, tk=tk, nkb=nkb,
                measure_keys=past > 0)

    pair = pl.program_id(1)

    def tile(qi, carry):
        _attn_tile(qi, pair, refs, mode=mode, tq=tq, tk=tk, nkb=nkb, past=past,
                   tk_valid=tk_valid, lam_init=lam_init)
        return carry

    lax.fori_loop(0, nq, tile, 0)


def _attn_tile(qi, pair, refs, *, mode, tq, tk, nkb, past, tk_valid, lam_init):
    if mode == "fox":
        qn_ref, q_ref, k_ref, v_ref, zs_ref, kx_ref, qc_ref, o_ref = refs[:8]
    elif mode == "diff":
        qn_ref, q_ref, k_ref, v_ref, zs_ref, lam_ref, subg_ref, o_ref = refs[:8]
    else:
        qn_ref, q_ref, k_ref, v_ref, zs_ref, o_ref = refs[:6]
    vt_ref, bound_ref, m_ref, l_ref, acc_ref = refs[-5:]
    vrows = acc_ref.shape[1]

    q0 = pl.multiple_of(qi * tq, tq)
    q_rows = pl.ds(q0, tq)
    row = lax.broadcasted_iota(jnp.int32, (LANES, 1), 0)
    low = row < (LANES // 2)

    qt = q_ref[q_rows, :].astype(F32).T
    if mode == "mla":
        rhs = (qt[0:LANES].astype(BF16), qt[LANES:2 * LANES].astype(BF16))
    else:
        rhs = (jnp.where(low, qt, 0.0).astype(BF16), jnp.where(low, 0.0, qt).astype(BF16))

    bounds = [jnp.full((1, tq), bound_ref[h], F32) for h in range(2)]
    fixed_offset = jnp.maximum(bound_ref[0], bound_ref[1]) <= FIXED_OFFSET_MAX_BOUND

    if mode == "fox":
        picks = []
        for h in range(2):
            first = 3 * (2 * pair + h)
            sel = jnp.logical_and(row >= first, row < first + 3)
            picks.append(jnp.broadcast_to(jnp.where(sel, -1.0, 0.0), (LANES, tq)).astype(BF16))
        rhs = tuple(jnp.concatenate([rhs[h], picks[h]], axis=0) for h in range(2))
        qc = qc_ref[qi] * LOG2E
        qcl = tuple(qc[h:h + 1, :] for h in range(2))
    else:
        qcl = (None, None)

    m_ref[...] = jnp.full(m_ref.shape, NEG_INF, F32)
    l_ref[...] = jnp.zeros(l_ref.shape, F32)
    acc_ref[...] = jnp.zeros(acc_ref.shape, F32)

    if mode == "mla":
        zero = jnp.zeros_like(rhs[0])
        rhs_all = jnp.concatenate([jnp.concatenate([rhs[0], zero], axis=0),
                                   jnp.concatenate([zero, rhs[1]], axis=0)], axis=1)
    else:
        rhs_all = jnp.concatenate(rhs, axis=1)
    q_pos = past + q0 + lax.broadcasted_iota(jnp.int32, (1, tq), 1)
    q_pos = jnp.concatenate([q_pos, q_pos], axis=1)
    bound_all = jnp.concatenate(bounds, axis=1)
    if mode == "fox":
        qcl = jnp.concatenate(qcl, axis=1)
        offset_all = bound_all - qcl
    else:
        offset_all = bound_all
    stream_cols = (slice(0, tq), slice(tq, 2 * tq))

    def step(j, masked, fixed, width=1):
        nk = width * tk
        k0 = pl.multiple_of(j * nk, nk)

        def values(h, w, lo, n):
            rows = slice(0, LANES) if mode == "diff" else slice(h * vrows, (h + 1) * vrows)
            return vt_ref[j * width + w, rows, lo:lo + n]

        def scores(lo, n):
            ks = pl.ds(pl.multiple_of(k0 + lo, n), n)
            if mode == "fox":
                lhs = jnp.concatenate([k_ref[ks, :], kx_ref[ks, :]], axis=1)
            else:
                lhs = k_ref[ks, :]
            s = _dot(lhs, rhs_all)
            if masked:
                head = n - tk
                k_pos = k0 + lo + head + lax.broadcasted_iota(jnp.int32, (tk, 1), 0)
                if mode == "fox":
                    mask = k_pos <= q_pos
                else:
                    mask = (k_pos >> CHUNK_SHIFT) <= (q_pos >> CHUNK_SHIFT)
                    if tk_valid < nkb * tk:
                        mask = jnp.logical_and(mask, k_pos < tk_valid)
                tail = jnp.where(mask, s[head:], NEG_INF)
                s = tail if head == 0 else jnp.concatenate([s[:head], tail], axis=0)
            return s

        s = scores(0, nk)
        vt = [jnp.concatenate([values(h, w, 0, tk) for w in range(width)], axis=1)
              if width > 1 else values(h, 0, 0, tk) for h in range(2)]
        if fixed:
            p = jnp.exp2(s - offset_all)
            l_ref[...] = l_ref[...] + jnp.sum(p, axis=0, keepdims=True)
            p = p.astype(BF16)
            for h in range(2):
                acc_ref[h] = acc_ref[h] + _dot(vt[h], p[:, stream_cols[h]])
            return
        m_old = m_ref[...]
        col_max = jnp.max(s, axis=0, keepdims=True)
        if mode == "fox":
            m_new = jnp.maximum(m_old, col_max + qcl)
            p = jnp.exp2(s - (m_new - qcl))
        else:
            m_new = jnp.maximum(m_old, col_max)
            p = jnp.exp2(s - m_new)
        alpha = jnp.exp2(m_old - m_new)
        l_ref[...] = alpha * l_ref[...] + jnp.sum(p, axis=0, keepdims=True)
        m_ref[...] = m_new
        p = p.astype(BF16)
        for h in range(2):
            cols = stream_cols[h]
            acc_ref[h] = alpha[:, cols] * acc_ref[h] + _dot(vt[h], p[:, cols])

    if mode == "fox":
        n_full = (past + q0 + 1) // tk
        j_last = (past + q0 + tq - 1) // tk
    else:
        n_full = ((past + q0) // CHUNK + 1) * CHUNK // tk
        j_last = ((past + q0 + tq - 1) // CHUNK * CHUNK + CHUNK - 1) // tk
    n_full = jnp.minimum(n_full, tk_valid // tk)
    j_last = jnp.minimum(j_last, nkb - 1)

    def sweep(fixed):
        def loop(lo, hi, masked, width=1):
            def body(j, carry):
                step(j, masked, fixed, width)
                return carry
            lax.fori_loop(lo, hi, body, 0)

        if not fixed:
            loop(0, n_full, False)
            loop(n_full, j_last + 1, True)
            return
        diag_in_chain = past == 0 and tq == tk
        blocks = j_last + 1 if diag_in_chain else n_full
        start = 0
        for width in FIXED_WIDTHS:
            if width > nkb:
                continue
            trips = (blocks - start) // width
            first = start // width
            start = start + trips * width
            if diag_in_chain:
                last = jnp.logical_and(trips > 0, start == blocks)
                loop(first, first + trips - last.astype(jnp.int32), False, width)
                pl.when(last)(functools.partial(step, first + trips - 1, True, True, width))
            else:
                loop(first, first + trips, False, width)
        if not diag_in_chain:
            loop(n_full, j_last + 1, True)

    @pl.when(fixed_offset)
    def _():
        sweep(True)

    @pl.when(jnp.logical_not(fixed_offset))
    def _():
        sweep(False)

    o0 = acc_ref[0] / l_ref[:, stream_cols[0]]
    o1 = acc_ref[1] / l_ref[:, stream_cols[1]]
    if mode == "diff":
        lv = lam_ref[...]
        lam = (jnp.exp(jnp.sum(lv[0:1] * lv[1:2], axis=-1, keepdims=True))
               - jnp.exp(jnp.sum(lv[2:3] * lv[3:4], axis=-1, keepdims=True)) + lam_init)
        ot = o0 - lam * o1
        ot = ot * lax.rsqrt(jnp.mean(ot * ot, axis=0, keepdims=True) + EPS)
        ot = ot * subg_ref[...] * (1.0 - lam_init)
    else:
        ot = jnp.concatenate([o0, o1], axis=0)
    o_ref[q_rows, :] = (ot.T * zs_ref[q_rows, :].astype(F32)).astype(BF16)


def _norm_bounds(q_gain, k_gain, width, qscale):
    root = math.sqrt(width)
    return jnp.stack([root * jnp.max(jnp.abs(q_gain)) * qscale,
                      root * jnp.max(jnp.abs(k_gain))]).reshape(1, 2).astype(F32)


def _attention(mode, q, k_all, v_all, zs, extras, *, past, tk_valid, norm_bounds, lam_init=0.0):
    b, t, d = zs.shape
    tk_total = k_all.shape[1]
    if past == 0:
        tq = min(ATTN_TILE, t)
        tk = tq
        t_run = t
    else:
        tq = LANES
        tk = tk_total
        t_run = -(-t // tq) * tq
        q = _pad_rows(q, t_run)
        zs = _pad_rows(zs, t_run)
    nkb = tk_total // tk
    groups = d // LANES
    qw = q.shape[2] // groups

    nq = t_run // tq
    in_specs = [
        pl.BlockSpec(memory_space=pltpu.SMEM),
        pl.BlockSpec((None, t_run, qw), lambda b_, g: (b_, 0, g)),
        pl.BlockSpec((None, tk_total, qw), lambda b_, g: (b_, 0, g)),
        pl.BlockSpec((None, tk_total, LANES), lambda b_, g: (b_, 0, g)),
        pl.BlockSpec((None, t_run, LANES), lambda b_, g: (b_, 0, g)),
    ]
    if mode == "fox":
        kx, qc = extras
        qc = jnp.pad(qc, ((0, 0), (0, 0), (0, 0), (0, t_run - t)), mode="edge")
        qc = jnp.transpose(qc.reshape(b, groups, 2, nq, tq), (0, 1, 3, 2, 4))
        extras = (kx, qc)
        in_specs += [
            pl.BlockSpec((None, tk_total, LANES), lambda b_, g: (b_, 0, 0)),
            pl.BlockSpec((None, None, nq, 2, tq), lambda b_, g: (b_, g, 0, 0, 0)),
        ]
    elif mode == "diff":
        lam_vecs, sub_g = extras
        in_specs += [
            pl.BlockSpec(lam_vecs.shape, lambda b_, g: (0, 0)),
            pl.BlockSpec(sub_g.shape, lambda b_, g: (0, 0)),
        ]
    kern = functools.partial(_attn_kernel, mode=mode, tq=tq, tk=tk, nkb=nkb, nq=nq, past=past,
                             tk_valid=tk_valid, lam_init=lam_init)
    u = pl.pallas_call(
        kern,
        grid=(b, groups),
        in_specs=in_specs,
        out_specs=pl.BlockSpec((None, t_run, LANES), lambda b_, g: (b_, 0, g)),
        out_shape=jax.ShapeDtypeStruct((b, t_run, d), BF16),
        scratch_shapes=[pltpu.VMEM((nkb, LANES, tk), BF16), pltpu.SMEM((2,), F32),
                        pltpu.VMEM((1, 2 * tq), F32), pltpu.VMEM((1, 2 * tq), F32),
                        pltpu.VMEM((2, LANES if mode == "diff" else LANES // 2, tq), F32)],
        compiler_params=_cparams("parallel", "parallel"),
        name=mode + "_attention",
    )(norm_bounds, q, k_all, v_all, zs, *extras)
    return u[:, :t]


def _group_matrix(group):
    idx = np.arange(MXU_DIM) // group
    return jnp.asarray((idx[:, None] == idx[None, :]).astype(np.float32), BF16)


def _rope_tables(pos, half, lead, trail):
    inv = ROPE_THETA ** (-jnp.arange(half, dtype=F32) / half)
    ang = pos.astype(F32)[:, None] * inv
    cos = jnp.cos(ang)
    sin = jnp.sin(ang)
    reps = (LANES - lead - trail) // (2 * half)
    n = pos.shape[0]
    cos_t = jnp.concatenate([jnp.ones((n, lead), F32)] + [cos, cos] * reps
                            + [jnp.ones((n, trail), F32)], axis=1)
    sin_t = jnp.concatenate([jnp.zeros((n, lead), F32)] + [-sin, sin] * reps
                            + [jnp.zeros((n, trail), F32)], axis=1)
    return cos_t, sin_t


def _pad_rows(x, rows, axis=1):
    if x.shape[axis] == rows:
        return x
    pad = [(0, 0)] * x.ndim
    pad[axis] = (0, rows - x.shape[axis])
    return jnp.pad(x, pad)


def _key_rows(past, t):
    if past == 0:
        return t
    return -(-(past + t) // DEC_KEY_TILE) * DEC_KEY_TILE


def _prep_weights(p):
    d = p["norm_g"].shape[1]
    w = {}
    a_in = p["a_in_w"]
    heads_a = d // A_HEAD_DIM
    w["a_in"] = jnp.concatenate(
        [a_in[:, :, :3 * d], a_in[:, :, 3 * d + heads_a:],
         jnp.pad(a_in[:, :, 3 * d:3 * d + heads_a], ((0, 0), (0, 0), (0, LANES - heads_a)))],
        axis=2).astype(BF16)
    w["a_fb"] = jnp.pad(p["a_f_b"], ((0, 0), (0, LANES - heads_a)))[:, None, :]
    w["a_qg"] = jnp.tile(p["a_q_g"], (1, heads_a))[:, None, :]
    w["a_kg"] = jnp.tile(p["a_k_g"], (1, heads_a))[:, None, :]

    heads_b = d // (2 * B_HEAD_DIM)
    w["b_in"] = p["b_in_w"].astype(BF16)
    nb = p["b_q_g"].shape[0]
    w["b_qg"] = jnp.tile(p["b_q_g"].reshape(nb, 2 * B_HEAD_DIM), (1, heads_b))[:, None, :]
    w["b_kg"] = jnp.tile(p["b_k_g"].reshape(nb, 2 * B_HEAD_DIM), (1, heads_b))[:, None, :]
    w["b_lam"] = jnp.stack([p["b_lam_q1"], p["b_lam_k1"], p["b_lam_q2"], p["b_lam_k2"]], axis=1)
    w["b_subg"] = p["b_sub_g"][:, :, None]

    c_in = p["c_in_w"]
    q_lora = p["c_qa_g"].shape[1]
    kv_lora = p["c_kva_g"].shape[1]
    n0 = q_lora + kv_lora
    w["c_in"] = jnp.concatenate(
        [c_in[:, :, :n0],
         jnp.pad(c_in[:, :, n0:n0 + C_ROPE], ((0, 0), (0, 0), (0, LANES - C_ROPE))),
         c_in[:, :, n0 + C_ROPE:]], axis=2).astype(BF16)
    w["c_qag"] = p["c_qa_g"][:, None, :]
    w["c_kvag"] = p["c_kva_g"][:, None, :]
    nc = c_in.shape[0]
    hd = C_NOPE + C_ROPE
    qb = p["c_qb_w"].reshape(nc, q_lora, C_HEADS, hd)
    w["c_qb"] = jnp.pad(qb, ((0, 0), (0, 0), (0, 0), (0, LANES - hd))).reshape(
        nc, q_lora, C_HEADS * LANES).astype(BF16)
    kvb = p["c_kvb_w"].reshape(nc, kv_lora, C_HEADS, C_NOPE + C_VDIM)
    w["c_kb"] = jnp.pad(kvb[..., :C_NOPE], ((0, 0), (0, 0), (0, 0), (0, LANES - C_NOPE))).reshape(
        nc, kv_lora, C_HEADS * LANES).astype(BF16)
    w["c_vb"] = kvb[..., C_NOPE:].reshape(nc, kv_lora, C_HEADS * C_VDIM).astype(BF16)
    slab_gain = lambda g: jnp.tile(jnp.pad(g, ((0, 0), (0, LANES - hd))), (1, C_HEADS))[:, None, :]
    w["c_qg"] = slab_gain(p["c_q_g"])
    w["c_kg"] = slab_gain(p["c_k_g"])
    w["out"] = p["out_w"].astype(BF16)
    return w


def _run_trunk(x, mods, caches, p, w, past):
    b, t, d = x.shape
    depth = p["norm_g"].shape[0]
    heads_a = d // A_HEAD_DIM
    tk_valid = past + t
    tk_total = _key_rows(past, t)
    q_pos = past + jnp.arange(t)
    k_pos = jnp.arange(tk_total)
    bd64 = _group_matrix(A_HEAD_DIM)
    bd128 = _group_matrix(LANES)
    new_a, new_b, new_c = [], [], []
    mods4 = mods.reshape(depth, b, 1, 3 * d)

    def with_past(cache, new_bf):
        if past == 0:
            return new_bf
        old = cache.reshape(b, past, -1).astype(BF16)
        return _pad_rows(jnp.concatenate([old, new_bf], axis=1), tk_total)

    for i in range(depth):
        j = i // 3
        shift, scale, gate = ((mods4, i, part) for part in range(3))
        g = p["norm_g"][i][None, :]
        if i % 3 == 0:
            qscale = A_HEAD_DIM ** -0.5 * LOG2E
            k, v, logf, qb, kb, vb, zs = _fox_proj(
                x, g, scale, shift, w["a_in"][j], w["a_fb"][j], w["a_qg"][j], w["a_kg"][j],
                bd64, qscale)
            new_a.append((k.reshape(b, t, heads_a, A_HEAD_DIM),
                          v.reshape(b, t, heads_a, A_HEAD_DIM), logf[:, :, :heads_a]))
            if past == 0:
                k_all, v_all, lf_all = kb, vb, logf
            else:
                k_all = with_past(caches[0][j], kb)
                v_all = with_past(caches[1][j], vb)
                lf_old = _pad_rows(caches[2][j], LANES, axis=2)
                lf_all = _pad_rows(jnp.concatenate([lf_old, logf], axis=1), tk_total)
            cum, kx = _cumsum_rows(lf_all, heads_a)
            qc = jnp.transpose(cum[:, past:past + t, :heads_a], (0, 2, 1)).reshape(
                b, heads_a // 2, 2, t)
            u = _attention("fox", qb, k_all, v_all, zs, (kx, qc), past=past, tk_valid=tk_valid,
                           norm_bounds=_norm_bounds(p["a_q_g"][j], p["a_k_g"][j], A_HEAD_DIM,
                                                    qscale))
        elif i % 3 == 1:
            qscale = B_HEAD_DIM ** -0.5 * LOG2E
            cos, sin = _rope_tables(q_pos, B_HEAD_DIM // 2, 0, 0)
            k, v, qb, kb, vb, zs = _diff_proj(
                x, g, scale, shift, w["b_in"][j], w["b_qg"][j], w["b_kg"][j], bd64, cos, sin,
                qscale)
            heads_b = d // (2 * B_HEAD_DIM)
            new_b.append((k.reshape(b, t, heads_b, 2, B_HEAD_DIM),
                          v.reshape(b, t, heads_b, 2 * B_HEAD_DIM)))
            k_all = with_past(None if past == 0 else caches[3][j], kb)
            v_all = with_past(None if past == 0 else caches[4][j], vb)
            lam_init = 0.8 - 0.6 * math.exp(-0.3 * i)
            u = _attention("diff", qb, k_all, v_all, zs, (w["b_lam"][j], w["b_subg"][j]),
                           past=past, tk_valid=tk_valid, lam_init=lam_init,
                           norm_bounds=_norm_bounds(p["b_q_g"][j], p["b_k_g"][j], B_HEAD_DIM,
                                                    qscale))
        else:
            qscale = (C_NOPE + C_ROPE) ** -0.5 * LOG2E
            cos_q, sin_q = _rope_tables(q_pos, C_ROPE // 2, C_NOPE, LANES - C_NOPE - C_ROPE)
            cos_k, sin_k = _rope_tables(k_pos, C_ROPE // 2, C_NOPE, LANES - C_NOPE - C_ROPE)
            kv_lat, kpe, qb, zs = _mla_proj(
                x, g, scale, shift, w["c_in"][j], w["c_qag"][j], w["c_kvag"][j], w["c_qb"][j],
                w["c_qg"][j], bd128, cos_q, sin_q, qscale)
            new_c.append((kv_lat, kpe))
            if past == 0:
                kv_all, kpe_all = kv_lat, kpe
            else:
                kv_all = _pad_rows(jnp.concatenate([caches[5][j], kv_lat], axis=1), tk_total)
                kpe_all = _pad_rows(jnp.concatenate([caches[6][j], kpe], axis=1), tk_total)
            kpe_slab = jnp.pad(kpe_all, ((0, 0), (0, 0), (C_NOPE, LANES - C_NOPE - C_ROPE)))
            k_all, v_all = _mla_key(kv_all, kpe_slab, w["c_kb"][j], w["c_vb"][j], w["c_kg"][j],
                                    bd128, cos_k, sin_k)
            u = _attention("mla", qb, k_all, v_all, zs, (), past=past, tk_valid=tk_valid,
                           norm_bounds=_norm_bounds(p["c_q_g"][j], p["c_k_g"][j],
                                                    C_NOPE + C_ROPE, qscale))
        x = _out_proj(x, u, w["out"][i], gate)

    stack = lambda rows: tuple(jnp.stack(r) for r in zip(*rows))
    return x, stack(new_a), stack(new_b), stack(new_c)


def kernel(x_prompt, x_sample, cache_a_k, cache_a_v, cache_a_logf, cache_b_k, cache_b_v, cache_c_kv, cache_c_kpe, c_prompt, c_sample, norm_g, ada_w, ada_b, out_w, a_in_w, a_f_b, a_q_g, a_k_g, b_in_w, b_q_g, b_k_g, b_lam_q1, b_lam_k1, b_lam_q2, b_lam_k2, b_sub_g, c_in_w, c_qa_g, c_kva_g, c_qb_w, c_kvb_w, c_q_g, c_k_g):
    p = dict(norm_g=norm_g, out_w=out_w, a_in_w=a_in_w, a_f_b=a_f_b, a_q_g=a_q_g, a_k_g=a_k_g,
             b_in_w=b_in_w, b_q_g=b_q_g, b_k_g=b_k_g, b_lam_q1=b_lam_q1, b_lam_k1=b_lam_k1,
             b_lam_q2=b_lam_q2, b_lam_k2=b_lam_k2, b_sub_g=b_sub_g, c_in_w=c_in_w,
             c_qa_g=c_qa_g, c_kva_g=c_kva_g, c_qb_w=c_qb_w, c_kvb_w=c_kvb_w, c_q_g=c_q_g,
             c_k_g=c_k_g)
    w = _prep_weights(p)
    bp = x_prompt.shape[0]
    mods = _ada(jnp.concatenate([c_prompt, c_sample], axis=0), ada_w, ada_b)
    caches = (cache_a_k, cache_a_v, cache_a_logf, cache_b_k, cache_b_v, cache_c_kv, cache_c_kpe)
    y_p, (a_k_p, a_v_p, a_lf_p), (b_k_p, b_v_p), (c_kv_p, c_kpe_p) = _run_trunk(
        x_prompt, mods[:, :bp], None, p, w, 0)
    y_s, (a_k_s, a_v_s, a_lf_s), (b_k_s, b_v_s), (c_kv_s, c_kpe_s) = _run_trunk(
        x_sample, mods[:, bp:], caches, p, w, cache_a_k.shape[2])
    return (y_p, y_s, a_k_p, a_v_p, a_lf_p, b_k_p, b_v_p, c_kv_p, c_kpe_p,
            a_k_s, a_v_s, a_lf_s, b_k_s, b_v_s, c_kv_s, c_kpe_s)
```
